```python
import jax, jax.numpy as jnp
from jax import lax
import numpy as np

D_MODEL = 1024
BATCH = 8
SEQ = 4096
DEPTH = 1

EPS = 1e-6
ROPE_THETA = 10000.0
ATT_HEADS = 8
ATT_HEAD_DIM = 64
IDX_HEADS = 8
IDX_HEAD_DIM = 64
TOPK_MAX = 256
Q_BLOCK = 64
RET_HEADS = 4
RET_QK_DIM = 128
RET_V_DIM = 256
RET_CHUNK = 128
D_FF = 2816
CONV_WIDTH = 3

ATT_W = ATT_HEADS * ATT_HEAD_DIM
IDX_QW = IDX_HEADS * IDX_HEAD_DIM
RET_QKW = RET_HEADS * RET_QK_DIM
RET_VW = RET_HEADS * RET_V_DIM
IN_SPLITS = (ATT_W, ATT_W, ATT_W,
             IDX_QW, IDX_HEAD_DIM, IDX_HEADS,
             RET_QKW, RET_QKW, RET_VW, RET_VW,
             D_MODEL, D_MODEL)
IN_WIDTH = int(sum(IN_SPLITS))

kernel_name = "hybrid_dsa_retention_convffn_block"


def rms_norm(x, g):
    xf = x.astype(jnp.float32)
    y = xf * lax.rsqrt(jnp.mean(xf * xf, axis=-1, keepdims=True) + EPS)
    return (y * g.astype(jnp.float32)).astype(x.dtype)


def rotary(x, positions):
    d = x.shape[-1]
    half = d // 2
    inv_freq = ROPE_THETA ** (-(jnp.arange(half, dtype=jnp.float32) * 2.0 / d))
    ang = positions.astype(jnp.float32)[..., None] * inv_freq
    cos = jnp.cos(ang)[:, :, None, :]
    sin = jnp.sin(ang)[:, :, None, :]
    xf = x.astype(jnp.float32)
    x1, x2 = xf[..., :half], xf[..., half:]
    out = jnp.concatenate([x1 * cos - x2 * sin, x2 * cos + x1 * sin], axis=-1)
    return out.astype(x.dtype)


def indexer_sparse_attention(q, k, v, q_idx, k_idx, w_idx):
    B, T, H, dh = q.shape
    topk = min(TOPK_MAX, T // 4)
    n_blocks = T // Q_BLOCK
    key_pos = jnp.arange(T)
    scale = dh ** -0.5

    def block(start):
        qi = lax.dynamic_slice_in_dim(q_idx, start, Q_BLOCK, axis=1)
        wi = lax.dynamic_slice_in_dim(w_idx, start, Q_BLOCK, axis=1)
        qb = lax.dynamic_slice_in_dim(q, start, Q_BLOCK, axis=1)
        q_pos = start + jnp.arange(Q_BLOCK)
        rel = jax.nn.relu(jnp.einsum('bqhd,bsd->bqhs', qi, k_idx))
        score = jnp.einsum('bqhs,bqh->bqs', rel, wi).astype(jnp.float32)
        causal = key_pos[None, :] <= q_pos[:, None]
        score = jnp.where(causal[None], score, -jnp.inf)
        _, idx = lax.top_k(score, topk)
        k_sel = jax.vmap(lambda kb, ib: kb[ib])(k, idx)
        v_sel = jax.vmap(lambda vb, ib: vb[ib])(v, idx)
        logits = jnp.einsum('bqhd,bqkhd->bhqk', qb, k_sel).astype(jnp.float32) * scale
        valid = (idx <= q_pos[None, :, None])[:, None]
        p = jax.nn.softmax(jnp.where(valid, logits, -jnp.inf), axis=-1).astype(v.dtype)
        return jnp.einsum('bhqk,bqkhd->bqhd', p, v_sel)

    starts = jnp.arange(n_blocks) * Q_BLOCK
    out = lax.map(block, starts)
    return jnp.transpose(out, (1, 0, 2, 3, 4)).reshape(B, T, H * dh)


def chunkwise_retention(q, k, v):
    B, T, Hr, dk = q.shape
    dv = v.shape[-1]
    C = RET_CHUNK
    N = T // C
    dt = q.dtype
    log_gamma = jnp.log(1.0 - 2.0 ** (-5.0 - jnp.arange(Hr, dtype=jnp.float32)))
    i = jnp.arange(C, dtype=jnp.float32)
    diff = i[:, None] - i[None, :]
    inner_decay = jnp.where(diff[None] >= 0,
                            jnp.exp(jnp.maximum(diff, 0.0)[None] * log_gamma[:, None, None]),
                            0.0).astype(dt)
    xi = jnp.exp((i + 1.0)[:, None] * log_gamma[None, :])[..., None].astype(dt)
    zeta = jnp.exp((C - 1.0 - i)[:, None] * log_gamma[None, :])[..., None].astype(dt)
    chunk_decay = jnp.exp(C * log_gamma).astype(dt)[:, None, None]

    def to_chunks(a):
        return jnp.transpose(a.reshape(B, N, C, Hr, a.shape[-1]), (1, 0, 2, 3, 4))

    def step(R, inp):
        qc, kc, vc = inp
        att = jnp.einsum('bihd,bjhd->bhij', qc, kc) * inner_decay
        inner = jnp.einsum('bhij,bjhe->bihe', att, vc)
        cross = jnp.einsum('bihd,bhde->bihe', qc, R) * xi
        R_new = R * chunk_decay + jnp.einsum('bjhd,bjhe->bhde', kc * zeta, vc)
        return R_new, inner + cross

    R0 = jnp.zeros((B, Hr, dk, dv), dtype=dt)
    _, ys = lax.scan(step, R0, (to_chunks(q), to_chunks(k), to_chunks(v)))
    return jnp.transpose(ys, (1, 0, 2, 3, 4)).reshape(B, T, Hr, dv)


def head_group_norm(o, gain):
    of = o.astype(jnp.float32)
    mu = jnp.mean(of, axis=-1, keepdims=True)
    var = jnp.mean(jnp.square(of - mu), axis=-1, keepdims=True)
    y = ((of - mu) * lax.rsqrt(var + EPS)).reshape(o.shape[0], o.shape[1], -1)
    return (y * gain.astype(jnp.float32)).astype(o.dtype)


def token_mixing(h, positions, w_in, w_br_attn, w_br_ret, ret_gn_gain, w_out):
    B, T, _ = h.shape
    proj = h @ w_in
    cuts = [int(c) for c in np.cumsum(IN_SPLITS)[:-1]]
    (aq, ak, av, iq, ik, iw, rq, rk, rv, rg, ga, gb) = jnp.split(proj, cuts, axis=-1)

    aq = rotary(aq.reshape(B, T, ATT_HEADS, ATT_HEAD_DIM), positions)
    ak = rotary(ak.reshape(B, T, ATT_HEADS, ATT_HEAD_DIM), positions)
    av = av.reshape(B, T, ATT_HEADS, ATT_HEAD_DIM)
    iq = rotary(iq.reshape(B, T, IDX_HEADS, IDX_HEAD_DIM), positions)
    ik = rotary(ik.reshape(B, T, 1, IDX_HEAD_DIM), positions)[:, :, 0]
    iw = iw * (IDX_HEADS ** -0.5 * IDX_HEAD_DIM ** -0.5)
    y_a = indexer_sparse_attention(aq, ak, av, iq, ik, iw) @ w_br_attn

    rq = rotary(rq.reshape(B, T, RET_HEADS, RET_QK_DIM), positions)
    rk = rotary(rk.reshape(B, T, RET_HEADS, RET_QK_DIM), positions) * (RET_QK_DIM ** -0.5)
    rv = rv.reshape(B, T, RET_HEADS, RET_V_DIM)
    ret = head_group_norm(chunkwise_retention(rq, rk, rv), ret_gn_gain)
    y_b = (jax.nn.silu(rg) * ret) @ w_br_ret

    merged = jax.nn.sigmoid(ga) * y_a + jax.nn.sigmoid(gb) * y_b
    return merged @ w_out


def conv_ffn(h, w_ffn_up, conv_w, conv_b, w_ffn_down):
    u = h @ w_ffn_up
    ch = u.shape[-1]
    u = lax.conv_general_dilated(u, conv_w[:, None, :], window_strides=(1,),
                                 padding=[(CONV_WIDTH - 1, 0)],
                                 dimension_numbers=('NWC', 'WIO', 'NWC'),
                                 feature_group_count=ch) + conv_b
    gate, up = jnp.split(u, 2, axis=-1)
    return (jax.nn.silu(gate) * up) @ w_ffn_down


def setup_inputs(seed: int = 0) -> dict:
    key = jax.random.key(seed)
    ks = jax.random.split(key, 16)
    f32 = jnp.float32

    def dense(k, fan_in, fan_out):
        return jax.random.normal(k, (DEPTH, fan_in, fan_out), f32) * fan_in ** -0.5

    def gain(k, n):
        return 1.0 + 0.05 * jax.random.normal(k, (DEPTH, n), f32)

    x = jax.random.normal(ks[0], (BATCH, SEQ, D_MODEL), f32)
    positions = jnp.broadcast_to(jnp.arange(SEQ, dtype=jnp.int32)[None], (BATCH, SEQ))
    return {
        "x": x,
        "positions": positions,
        "norm_pre_mix": gain(ks[1], D_MODEL),
        "w_in": dense(ks[2], D_MODEL, IN_WIDTH),
        "w_br_attn": dense(ks[3], ATT_W, D_MODEL),
        "w_br_ret": dense(ks[4], RET_VW, D_MODEL),
        "ret_gn_gain": gain(ks[5], RET_VW),
        "w_out": dense(ks[6], D_MODEL, D_MODEL),
        "norm_post_mix": gain(ks[7], D_MODEL),
        "norm_pre_ffn": gain(ks[8], D_MODEL),
        "w_ffn_up": dense(ks[9], D_MODEL, 2 * D_FF),
        "conv_w": jax.random.normal(ks[10], (DEPTH, CONV_WIDTH, 2 * D_FF), f32) * CONV_WIDTH ** -0.5,
        "conv_b": 0.02 * jax.random.normal(ks[11], (DEPTH, 2 * D_FF), f32),
        "w_ffn_down": dense(ks[12], D_FF, D_MODEL),
        "norm_post_ffn": gain(ks[13], D_MODEL),
    }


def reference(x, positions, norm_pre_mix, w_in, w_br_attn, w_br_ret, ret_gn_gain, w_out,
              norm_post_mix, norm_pre_ffn, w_ffn_up, conv_w, conv_b, w_ffn_down, norm_post_ffn):
    for l in range(DEPTH):
        h = rms_norm(x, norm_pre_mix[l])
        m = token_mixing(h, positions, w_in[l], w_br_attn[l], w_br_ret[l], ret_gn_gain[l], w_out[l])
        x = x + rms_norm(m, norm_post_mix[l])
        h = rms_norm(x, norm_pre_ffn[l])
        f = conv_ffn(h, w_ffn_up[l], conv_w[l], conv_b[l], w_ffn_down[l])
        x = x + rms_norm(f, norm_post_ffn[l])
    return x
```

```python
import functools

import jax
import jax.numpy as jnp
import numpy as np
from jax import lax
from jax.experimental import pallas as pl
from jax.experimental.pallas import tpu as pltpu

EPS = 1e-6
ROPE_THETA = 10000.0
ATT_HEADS = 8
ATT_HEAD_DIM = 64
IDX_HEADS = 8
IDX_HEAD_DIM = 64
TOPK_MAX = 256
RET_HEADS = 4
RET_QK_DIM = 128
RET_V_DIM = 256
RET_CHUNK = 128
CONV_WIDTH = 3

ATT_W = ATT_HEADS * ATT_HEAD_DIM
IDX_QW = IDX_HEADS * IDX_HEAD_DIM
RET_QKW = RET_HEADS * RET_QK_DIM
RET_VW = RET_HEADS * RET_V_DIM

LANES = 128
SUBLANES = 8
VMEM_LIMIT = 56 * 1024 * 1024

ROW_TILE = 512
TABLE_TILE = 1024
ATT_TQ = 256
ATT_TK = 256
CNT_ROWS = 32
RET_ROWS = 512
FFN_CHUNK = 256

F32 = jnp.float32
BF16 = jnp.bfloat16
NEG_INF = float("-inf")
F32_LOWEST = float(np.finfo(np.float32).min)
INT32_MIN = -(2 ** 31)


def _params(*semantics):
    return pltpu.CompilerParams(dimension_semantics=semantics, vmem_limit_bytes=VMEM_LIMIT)


def _resident(shape):
    zeros = (0,) * len(shape)
    return pl.BlockSpec(shape, lambda *_: zeros, pipeline_mode=pl.Buffered(1))


def _rms(x, gain):
    return x * lax.rsqrt(jnp.mean(x * x, axis=-1, keepdims=True) + EPS) * gain


def _rope_table_kernel(pos_ref, invf_ref, sgn64_ref, sgn128_ref, c64_ref, s64_ref, c128_ref, s128_ref):
    ang = pos_ref[...].astype(F32) * invf_ref[...]
    c = jnp.cos(ang)
    s = jnp.sin(ang)
    c_lo, c_hi = c[:, :64], c[:, 64:]
    s_lo, s_hi = s[:, :64], s[:, 64:]
    c64_ref[...] = jnp.concatenate([c_lo, c_lo], axis=1)
    s64_ref[...] = jnp.concatenate([s_lo, s_lo], axis=1) * sgn64_ref[...]
    c128_ref[...] = jnp.concatenate([c_hi, c_hi], axis=1)
    s128_ref[...] = jnp.concatenate([s_hi, s_hi], axis=1) * sgn128_ref[...]


def _rope_tables(positions):
    n = positions.size
    tm = min(TABLE_TILE, n)
    half64, half128 = ATT_HEAD_DIM // 2, RET_QK_DIM // 2
    f64 = ROPE_THETA ** (-(jnp.arange(half64, dtype=F32) * 2.0 / ATT_HEAD_DIM))
    f128 = ROPE_THETA ** (-(jnp.arange(half128, dtype=F32) * 2.0 / RET_QK_DIM))
    invf = jnp.concatenate([f64, f64, f128])[None, :]
    sgn64 = jnp.tile(jnp.concatenate([-jnp.ones(half64, F32), jnp.ones(half64, F32)]), 2)[None, :]
    sgn128 = jnp.concatenate([-jnp.ones(half128, F32), jnp.ones(half128, F32)])[None, :]
    row = pl.BlockSpec((tm, LANES), lambda i: (i, 0))
    const = pl.BlockSpec((1, LANES), lambda i: (0, 0))
    table = jax.ShapeDtypeStruct((n, LANES), F32)
    return pl.pallas_call(
        _rope_table_kernel,
        grid=(n // tm,),
        in_specs=[pl.BlockSpec((tm, 1), lambda i: (i, 0)), const, const, const],
        out_specs=[row, row, row, row],
        out_shape=[table, table, table, table],
        compiler_params=_params("parallel"),
        name="rope_tables",
    )(positions.reshape(n, 1), invf, sgn64, sgn128)


_C_AQ = 0
_C_AK = _C_AQ + ATT_W
_C_AV = _C_AK + ATT_W
_C_IQ = _C_AV + ATT_W
_C_IK = _C_IQ + IDX_QW
_C_IW = _C_IK + LANES
_C_RQ = _C_IW + LANES
_C_RK = _C_RQ + RET_QKW
_C_RV = _C_RK + RET_QKW
_C_RG = _C_RV + RET_VW
_C_GA = _C_RG + RET_VW


def _pack_w_in(w_in, d_model):
    splits = (ATT_W, ATT_W, ATT_W, IDX_QW, IDX_HEAD_DIM, IDX_HEADS,
              RET_QKW, RET_QKW, RET_VW, RET_VW, d_model, d_model)
    cuts = np.cumsum(splits)[:-1].tolist()
    aq, ak, av, iq, ik, iw, rq, rk, rv, rg, ga, gb = jnp.split(w_in, cuts, axis=1)
    iw_pad = jnp.pad(iw, ((0, 0), (0, LANES - IDX_HEADS)))
    packed = jnp.concatenate([aq, ak, av, iq, ik, ik, iw_pad, rq, rk, rv, rg, ga, gb], axis=1)
    return packed.astype(BF16)


def _in_proj_kernel(x_ref, g_ref, w_ref, c64_ref, s64_ref, c128_ref, s128_ref,
                    aq_ref, ak_ref, avt_ref, iq_ref, ik_ref, iw_ref,
                    rq_ref, rk_ref, rv_ref, rgs_ref, sa_ref, sb_ref, *, d_model):
    tm = x_ref.shape[0]
    h = _rms(x_ref[...], g_ref[...]).astype(BF16)

    def proj(c0, width):
        return jnp.dot(h, w_ref[:, c0:c0 + width], preferred_element_type=F32)

    lane = lax.broadcasted_iota(jnp.int32, (tm, LANES), 1)
    first_half = (lane % ATT_HEAD_DIM) < (ATT_HEAD_DIM // 2)
    c64, s64 = c64_ref[...], s64_ref[...]
    c128, s128 = c128_ref[...], s128_ref[...]

    def rot64(y):
        out = []
        for j in range(y.shape[1] // LANES):
            yj = y[:, j * LANES:(j + 1) * LANES]
            partner = jnp.where(first_half, pltpu.roll(yj, LANES - 32, 1), pltpu.roll(yj, 32, 1))
            out.append(yj * c64 + partner * s64)
        return out[0] if len(out) == 1 else jnp.concatenate(out, axis=1)

    def rot128(y):
        out = []
        for j in range(y.shape[1] // LANES):
            yj = y[:, j * LANES:(j + 1) * LANES]
            out.append(yj * c128 + pltpu.roll(yj, 64, 1) * s128)
        return jnp.concatenate(out, axis=1)

    aq_ref[0] = (rot64(proj(_C_AQ, ATT_W)) * (ATT_HEAD_DIM ** -0.5)).astype(BF16)
    ak_ref[0] = rot64(proj(_C_AK, ATT_W)).astype(BF16)
    avt_ref[0] = proj(_C_AV, ATT_W).T.astype(BF16)
    iq_ref[0] = rot64(proj(_C_IQ, IDX_QW)).astype(BF16)
    ik_ref[0] = rot64(proj(_C_IK, LANES)).astype(BF16)
    iw_ref[0] = proj(_C_IW, LANES) * (IDX_HEADS ** -0.5 * IDX_HEAD_DIM ** -0.5)
    rq_ref[0] = rot128(proj(_C_RQ, RET_QKW)).astype(BF16)
    rk_ref[0] = (rot128(proj(_C_RK, RET_QKW)) * (RET_QK_DIM ** -0.5)).astype(BF16)
    rv_ref[0] = proj(_C_RV, RET_VW).astype(BF16)
    rg = proj(_C_RG, RET_VW)
    rgs_ref[0] = rg * jax.nn.sigmoid(rg)
    sa_ref[0] = jax.nn.sigmoid(proj(_C_GA, d_model))
    sb_ref[0] = jax.nn.sigmoid(proj(_C_GA + d_model, d_model))


def _in_proj(x, gain, w_packed, tables):
    b, t, d = x.shape
    tm = min(ROW_TILE, t)
    nt = t // tm
    width = w_packed.shape[1]
    tok = lambda w: pl.BlockSpec((1, tm, w), lambda bi, ti: (bi, ti, 0))
    tab = pl.BlockSpec((tm, LANES), lambda bi, ti: (bi * nt + ti, 0))
    sds = lambda w, dt: jax.ShapeDtypeStruct((b, t, w), dt)
    outs = [
        (tok(ATT_W), sds(ATT_W, BF16)),
        (tok(ATT_W), sds(ATT_W, BF16)),
        (pl.BlockSpec((1, ATT_W, tm), lambda bi, ti: (bi, 0, ti)),
         jax.ShapeDtypeStruct((b, ATT_W, t), BF16)),
        (tok(IDX_QW), sds(IDX_QW, BF16)),
        (tok(LANES), sds(LANES, BF16)),
        (tok(LANES), sds(LANES, F32)),
        (tok(RET_QKW), sds(RET_QKW, BF16)),
        (tok(RET_QKW), sds(RET_QKW, BF16)),
        (tok(RET_VW), sds(RET_VW, BF16)),
        (tok(RET_VW), sds(RET_VW, F32)),
        (tok(d), sds(d, F32)),
        (tok(d), sds(d, F32)),
    ]
    return pl.pallas_call(
        functools.partial(_in_proj_kernel, d_model=d),
        grid=(b, nt),
        in_specs=[pl.BlockSpec((None, tm, d), lambda bi, ti: (bi, ti, 0)),
                  _resident((1, d)), _resident((d, width)), tab, tab, tab, tab],
        out_specs=[o[0] for o in outs],
        out_shape=[o[1] for o in outs],
        compiler_params=_params("parallel", "parallel"),
        name="in_proj",
    )(x, gain, w_packed, *tables)


def _sparse_attn_kernel(iq_ref, iw_ref, aq_ref, ik_ref, ak_ref, avt_ref, out_ref,
                        s_ref, qi_ref, qa_ref, o_ref, *, topk, seq_len):
    tq, tk = ATT_TQ, ATT_TK
    qi = pl.program_id(1)
    nk = qi + 1

    def krows(kc):
        return pl.ds(pl.multiple_of(kc * tk, tk), tk)

    lane = lax.broadcasted_iota(jnp.int32, (tq, LANES), 1)
    for h in range(ATT_HEADS):
        pair, odd = divmod(h, 2)
        keep = (lane >= ATT_HEAD_DIM) if odd else (lane < ATT_HEAD_DIM)
        cols = slice(pair * LANES, (pair + 1) * LANES)
        qi_ref[h] = jnp.where(keep, iq_ref[0, :, cols], jnp.zeros((tq, LANES), BF16))
        qa_ref[h] = jnp.where(keep, aq_ref[0, :, cols], jnp.zeros((tq, LANES), BF16))
    w_t = iw_ref[0].T

    for h in range(IDX_HEADS):
        q_h = qi_ref[h]
        w_h = w_t[h:h + 1, :]

        def score_body(kc, carry, q_h=q_h, w_h=w_h, first=(h == 0)):
            k = ik_ref[0, krows(kc), :]
            dots = lax.dot_general(k, q_h, (((1,), (1,)), ((), ())), preferred_element_type=F32)
            term = w_h * jnp.maximum(dots, 0.0)
            if first:
                s_ref[krows(kc), :] = term
            else:
                s_ref[krows(kc), :] += term
            return carry

        lax.fori_loop(0, nk, score_body, 0)

    key_in = lax.broadcasted_iota(jnp.int32, (tk, tq), 0)
    qry_in = lax.broadcasted_iota(jnp.int32, (tk, tq), 1)
    s_ref[krows(qi), :] = jnp.where(key_in <= qry_in, s_ref[krows(qi), :], NEG_INF)

    def count(hits):
        def body(kc, acc):
            hit = hits(s_ref[krows(kc), :], kc)
            return acc + hit.reshape(tk // CNT_ROWS, CNT_ROWS, tq).sum(axis=0)
        acc = lax.fori_loop(0, nk, body, jnp.zeros((CNT_ROWS, tq), jnp.int32))
        return acc.sum(axis=0, keepdims=True)

    def key_to_float(key):
        bits = jnp.where(key >= 0, key, key ^ jnp.int32(0x7FFFFFFF))
        return lax.bitcast_convert_type(bits, F32)

    def search_body(p, carry):
        cur, cnt_cur = carry
        cand = cur + lax.shift_left(jnp.int32(1), 31 - p)
        cand_f = key_to_float(cand)
        tot = count(lambda s, kc: jnp.where(s >= cand_f, 1, 0))
        ok = tot >= topk
        return jnp.where(ok, cand, cur), jnp.where(ok, tot, cnt_cur)

    cur0 = jnp.full((1, tq), INT32_MIN, jnp.int32)
    cur, cnt_cur = lax.fori_loop(0, 32, search_body, (cur0, jnp.zeros((1, tq), jnp.int32)))
    found = cur != INT32_MIN
    tau = jnp.where(found, key_to_float(cur), F32_LOWEST)

    tied = jnp.where(found & (cnt_cur > topk), 1, 0)

    @pl.when(jnp.max(tied) > 0)
    def _():
        need = topk - count(lambda s, kc: jnp.where(s > tau, 1, 0))

        def key_index(kc):
            return kc * tk + lax.broadcasted_iota(jnp.int32, (tk, tq), 0)

        def idx_body(p, bound):
            cand = bound + lax.shift_left(jnp.int32(1), seq_len.bit_length() - 1 - p)
            below = count(lambda s, kc: jnp.where(s == tau, jnp.where(key_index(kc) < cand, 1, 0), 0))
            return jnp.where(below < need, cand, bound)

        bound = lax.fori_loop(0, seq_len.bit_length(), idx_body, jnp.zeros((1, tq), jnp.int32))

        def drop_body(kc, carry):
            s = s_ref[krows(kc), :]
            s_ref[krows(kc), :] = jnp.where(s == tau, jnp.where(key_index(kc) > bound, NEG_INF, s), s)
            return carry

        lax.fori_loop(0, nk, drop_body, 0)

    def bias_body(kc, carry):
        s_ref[krows(kc), :] = jnp.where(s_ref[krows(kc), :] >= tau, 0.0, NEG_INF)
        return carry

    lax.fori_loop(0, nk, bias_body, 0)

    for h in range(ATT_HEADS):
        pair = h // 2
        q_h = qa_ref[h]

        def attn_body(kc, carry, q_h=q_h, pair=pair, h=h):
            m, l, acc = carry
            k = ak_ref[0, krows(kc), pair * LANES:(pair + 1) * LANES]
            logits = lax.dot_general(k, q_h, (((1,), (1,)), ((), ())), preferred_element_type=F32)
            logits = logits + s_ref[krows(kc), :]
            m_new = jnp.maximum(m, logits.max(axis=0, keepdims=True))
            m_safe = jnp.where(m_new == NEG_INF, 0.0, m_new)
            p = jnp.exp(logits - m_safe)
            alpha = jnp.exp(m - m_safe)
            l = alpha * l + p.sum(axis=0, keepdims=True)
            v_t = avt_ref[0, h * ATT_HEAD_DIM:(h + 1) * ATT_HEAD_DIM, krows(kc)]
            acc = alpha * acc + jnp.dot(v_t, p.astype(BF16), preferred_element_type=F32)
            return m_new, l, acc

        init = (jnp.full((1, tq), NEG_INF, F32), jnp.zeros((1, tq), F32),
                jnp.zeros((ATT_HEAD_DIM, tq), F32))
        _, l, acc = lax.fori_loop(0, nk, attn_body, init)
        o_ref[h * ATT_HEAD_DIM:(h + 1) * ATT_HEAD_DIM, :] = acc / l

    out_ref[0] = o_ref[...].T.astype(BF16)


def _sparse_attention(aq, ak, avt, iq, ik2, iw):
    b, t, _ = aq.shape
    tq = ATT_TQ
    assert t % tq == 0 and ATT_TQ == ATT_TK
    topk = min(TOPK_MAX, t // 4)
    qblk = lambda w: pl.BlockSpec((1, tq, w), lambda bi, qi: (bi, qi, 0))
    full = lambda r, c: pl.BlockSpec((1, r, c), lambda bi, qi: (bi, 0, 0))
    return pl.pallas_call(
        functools.partial(_sparse_attn_kernel, topk=topk, seq_len=t),
        grid=(b, t // tq),
        in_specs=[qblk(IDX_QW), qblk(LANES), qblk(ATT_W),
                  full(t, LANES), full(t, ATT_W), full(ATT_W, t)],
        out_specs=qblk(ATT_W),
        out_shape=jax.ShapeDtypeStruct((b, t, ATT_W), BF16),
        scratch_shapes=[
            pltpu.VMEM((t, tq), F32),
            pltpu.VMEM((IDX_HEADS, tq, LANES), BF16),
            pltpu.VMEM((ATT_HEADS, tq, LANES), BF16),
            pltpu.VMEM((ATT_W, tq), F32),
        ],
        compiler_params=_params("parallel", "parallel"),
        name="sparse_attn",
    )(iq, iw, aq, ik2, ak, avt)


def _retention_kernel(rq_ref, rk_ref, rv_ref, rgs_ref, gain_ref, decay_ref, xi_ref, zeta_ref, cd_ref,
                      out_ref, state_ref):
    c = RET_CHUNK

    @pl.when(pl.program_id(1) == 0)
    def _():
        state_ref[...] = jnp.zeros_like(state_ref)

    def chunk_body(ci, carry):
        rows = pl.ds(pl.multiple_of(ci * c, c), c)
        for h in range(RET_HEADS):
            qk_cols = slice(h * RET_QK_DIM, (h + 1) * RET_QK_DIM)
            v_cols = slice(h * RET_V_DIM, (h + 1) * RET_V_DIM)
            q = rq_ref[0, rows, qk_cols]
            k = rk_ref[0, rows, qk_cols]
            v = rv_ref[0, rows, v_cols]
            state = state_ref[h]
            att = lax.dot_general(q, k, (((1,), (1,)), ((), ())), preferred_element_type=F32)
            att = att * decay_ref[h]
            inner = jnp.dot(att.astype(BF16), v, preferred_element_type=F32)
            cross = jnp.dot(q, state.astype(BF16), preferred_element_type=F32) * xi_ref[h]
            kz_t = (k.astype(F32) * zeta_ref[h]).T.astype(BF16)
            state_ref[h] = state * cd_ref[h, 0:1, :] + jnp.dot(kz_t, v, preferred_element_type=F32)
            o = inner + cross
            mu = jnp.mean(o, axis=-1, keepdims=True)
            dev = o - mu
            var = jnp.mean(dev * dev, axis=-1, keepdims=True)
            y = dev * lax.rsqrt(var + EPS) * gain_ref[:, v_cols]
            out_ref[0, rows, v_cols] = (rgs_ref[0, rows, v_cols] * y).astype(BF16)
        return carry

    lax.fori_loop(0, rq_ref.shape[1] // c, chunk_body, 0)


def _retention(rq, rk, rv, rgs, gn_gain):
    b, t, _ = rq.shape
    rb = min(RET_ROWS, t)
    c = RET_CHUNK
    log_gamma = jnp.log(1.0 - 2.0 ** (-5.0 - jnp.arange(RET_HEADS, dtype=F32)))
    i = jnp.arange(c, dtype=F32)
    diff = i[:, None] - i[None, :]
    decay = jnp.where(diff[None] >= 0,
                      jnp.exp(jnp.maximum(diff, 0.0)[None] * log_gamma[:, None, None]), 0.0)
    xi = jnp.exp((i + 1.0)[None, :] * log_gamma[:, None])
    zeta = jnp.exp((c - 1.0 - i)[None, :] * log_gamma[:, None])
    xi_b = jnp.broadcast_to(xi[:, :, None], (RET_HEADS, c, RET_V_DIM))
    zeta_b = jnp.broadcast_to(zeta[:, :, None], (RET_HEADS, c, RET_QK_DIM))
    cd_b = jnp.broadcast_to(jnp.exp(c * log_gamma)[:, None, None], (RET_HEADS, SUBLANES, RET_V_DIM))
    tok = lambda w: pl.BlockSpec((1, rb, w), lambda bi, ti: (bi, ti, 0))
    return pl.pallas_call(
        _retention_kernel,
        grid=(b, t // rb),
        in_specs=[tok(RET_QKW), tok(RET_QKW), tok(RET_VW), tok(RET_VW), _resident((1, RET_VW)),
                  _resident(decay.shape), _resident(xi_b.shape), _resident(zeta_b.shape),
                  _resident(cd_b.shape)],
        out_specs=tok(RET_VW),
        out_shape=jax.ShapeDtypeStruct((b, t, RET_VW), BF16),
        scratch_shapes=[pltpu.VMEM((RET_HEADS, RET_QK_DIM, RET_V_DIM), F32)],
        compiler_params=_params("parallel", "arbitrary"),
        name="retention",
    )(rq, rk, rv, rgs, gn_gain, decay, xi_b, zeta_b, cd_b)


def _mix_out_kernel(attn_ref, gret_ref, sa_ref, sb_ref, x_ref, wa_ref, wr_ref, wo_ref,
                    g_post_ref, g_pre_ref, x1_ref, h2_ref):
    y_a = jnp.dot(attn_ref[...], wa_ref[...], preferred_element_type=F32)
    y_b = jnp.dot(gret_ref[...], wr_ref[...], preferred_element_type=F32)
    merged = sa_ref[...] * y_a + sb_ref[...] * y_b
    m = jnp.dot(merged.astype(BF16), wo_ref[...], preferred_element_type=F32)
    x1 = x_ref[...] + _rms(m, g_post_ref[...])
    x1_ref[...] = x1
    h2_ref[...] = _rms(x1, g_pre_ref[...]).astype(BF16)


def _mix_out(attn, gret, sa, sb, x, w_br_attn, w_br_ret, w_out, g_post, g_pre_ffn):
    b, t, d = x.shape
    tm = min(ROW_TILE, t)
    tok = lambda w: pl.BlockSpec((None, tm, w), lambda bi, ti: (bi, ti, 0))
    return pl.pallas_call(
        _mix_out_kernel,
        grid=(b, t // tm),
        in_specs=[tok(ATT_W), tok(RET_VW), tok(d), tok(d), tok(d),
                  _resident(w_br_attn.shape), _resident(w_br_ret.shape), _resident(w_out.shape),
                  _resident((1, d)), _resident((1, d))],
        out_specs=[tok(d), tok(d)],
        out_shape=[jax.ShapeDtypeStruct((b, t, d), F32), jax.ShapeDtypeStruct((b, t, d), BF16)],
        compiler_params=_params("parallel", "parallel"),
        name="mix_out",
    )(attn, gret, sa, sb, x, w_br_attn, w_br_ret, w_out, g_post, g_pre_ffn)


def _conv_ffn_kernel(h_ref, x1_ref, wup_ref, cw_ref, cb_ref, wdn_ref, g_ref, out_ref, carry_ref, *, d_ff):
    tm = h_ref.shape[0]

    @pl.when(pl.program_id(1) == 0)
    def _():
        carry_ref[...] = jnp.zeros_like(carry_ref)

    h = h_ref[...]
    row = lax.broadcasted_iota(jnp.int32, (tm, FFN_CHUNK), 0)

    def conv(c0):
        cols = slice(c0, c0 + FFN_CHUNK)
        u = jnp.dot(h, wup_ref[:, cols], preferred_element_type=F32)
        prev = carry_ref[:, cols]
        u1 = jnp.where(row == 0, prev[SUBLANES - 1:SUBLANES, :], pltpu.roll(u, 1, 0))
        u2 = jnp.where(row == 0, prev[SUBLANES - 2:SUBLANES - 1, :],
                       jnp.where(row == 1, prev[SUBLANES - 1:SUBLANES, :], pltpu.roll(u, 2, 0)))
        carry_ref[:, cols] = u[tm - SUBLANES:, :]
        return (cw_ref[0:1, cols] * u2 + cw_ref[1:2, cols] * u1 + cw_ref[2:3, cols] * u
                + cb_ref[:, cols])

    acc = jnp.zeros((tm, out_ref.shape[1]), F32)
    for c0 in range(0, d_ff, FFN_CHUNK):
        gate = conv(c0)
        up = conv(d_ff + c0)
        act = (gate * jax.nn.sigmoid(gate) * up).astype(BF16)
        acc = acc + jnp.dot(act, wdn_ref[c0:c0 + FFN_CHUNK, :], preferred_element_type=F32)
    out_ref[...] = x1_ref[...] + _rms(acc, g_ref[...])


def _conv_ffn(h2, x1, w_up, conv_w, conv_b, w_down, g_post):
    b, t, d = x1.shape
    d_ff = w_down.shape[0]
    assert d_ff % FFN_CHUNK == 0 and conv_w.shape[0] == CONV_WIDTH
    tm = min(ROW_TILE, t)
    tok = pl.BlockSpec((None, tm, d), lambda bi, ti: (bi, ti, 0))
    return pl.pallas_call(
        functools.partial(_conv_ffn_kernel, d_ff=d_ff),
        grid=(b, t // tm),
        in_specs=[tok, tok, _resident(w_up.shape), _resident(conv_w.shape), _resident((1, 2 * d_ff)),
                  _resident(w_down.shape), _resident((1, d))],
        out_specs=tok,
        out_shape=jax.ShapeDtypeStruct((b, t, d), F32),
        scratch_shapes=[pltpu.VMEM((SUBLANES, 2 * d_ff), F32)],
        compiler_params=_params("parallel", "arbitrary"),
        name="conv_ffn",
    )(h2, x1, w_up, conv_w, conv_b, w_down, g_post)


def _layer(x, positions, norm_pre_mix, w_in, w_br_attn, w_br_ret, ret_gn_gain, w_out,
           norm_post_mix, norm_pre_ffn, w_ffn_up, conv_w, conv_b, w_ffn_down, norm_post_ffn):
    d = x.shape[-1]
    row = lambda v: v.reshape(1, -1)
    tables = _rope_tables(positions)
    (aq, ak, avt, iq, ik2, iw, rq, rk, rv, rgs, sa, sb) = _in_proj(
        x, row(norm_pre_mix), _pack_w_in(w_in, d), tables)
    attn = _sparse_attention(aq, ak, avt, iq, ik2, iw)
    gret = _retention(rq, rk, rv, rgs, row(ret_gn_gain))
    x1, h2 = _mix_out(attn, gret, sa, sb, x, w_br_attn.astype(BF16), w_br_ret.astype(BF16),
                      w_out.astype(BF16), row(norm_post_mix), row(norm_pre_ffn))
    return _conv_ffn(h2, x1, w_ffn_up.astype(BF16), conv_w, row(conv_b), w_ffn_down.astype(BF16),
                     row(norm_post_ffn))


def kernel(x, positions, norm_pre_mix, w_in, w_br_attn, w_br_ret, ret_gn_gain, w_out, norm_post_mix,
           norm_pre_ffn, w_ffn_up, conv_w, conv_b, w_ffn_down, norm_post_ffn):
    for l in range(w_in.shape[0]):
        x = _layer(x, positions, norm_pre_mix[l], w_in[l], w_br_attn[l], w_br_ret[l], ret_gn_gain[l],
                   w_out[l], norm_post_mix[l], norm_pre_ffn[l], w_ffn_up[l], conv_w[l], conv_b[l],
                   w_ffn_down[l], norm_post_ffn[l])
    return x
```

```python
import functools

import jax
import jax.numpy as jnp
import numpy as np
from jax import lax
from jax.experimental import pallas as pl
from jax.experimental.pallas import tpu as pltpu

EPS = 1e-6
ROPE_THETA = 10000.0
ATT_HEADS = 8
ATT_HEAD_DIM = 64
IDX_HEADS = 8
IDX_HEAD_DIM = 64
TOPK_MAX = 256
RET_HEADS = 4
RET_QK_DIM = 128
RET_V_DIM = 256
RET_CHUNK = 128
CONV_WIDTH = 3

ATT_W = ATT_HEADS * ATT_HEAD_DIM
IDX_QW = IDX_HEADS * IDX_HEAD_DIM
RET_QKW = RET_HEADS * RET_QK_DIM
RET_VW = RET_HEADS * RET_V_DIM

LANES = 128
SUBLANES = 8
VMEM_LIMIT = 56 * 1024 * 1024

ROW_TILE = 512
TABLE_TILE = 1024
ATT_TQ = 256
ATT_TK = 256
CNT_ROWS = 32
SEARCH_UNROLL = 4
RET_ROWS = 512
FFN_CHUNK = 256

F32 = jnp.float32
BF16 = jnp.bfloat16
NEG_INF = float("-inf")
F32_LOWEST = float(np.finfo(np.float32).min)
F32_TINY = float(np.finfo(np.float32).tiny)
INT32_MIN = -(2 ** 31)


def _params(*semantics):
    return pltpu.CompilerParams(dimension_semantics=semantics, vmem_limit_bytes=VMEM_LIMIT)


def _resident(shape):
    zeros = (0,) * len(shape)
    return pl.BlockSpec(shape, lambda *_: zeros, pipeline_mode=pl.Buffered(1))


def _rms(x, gain):
    return x * lax.rsqrt(jnp.mean(x * x, axis=-1, keepdims=True) + EPS) * gain


def _rope_table_kernel(pos_ref, invf_ref, sgn64_ref, sgn128_ref, c64_ref, s64_ref, c128_ref, s128_ref):
    ang = pos_ref[...].astype(F32) * invf_ref[...]
    c = jnp.cos(ang)
    s = jnp.sin(ang)
    c_lo, c_hi = c[:, :64], c[:, 64:]
    s_lo, s_hi = s[:, :64], s[:, 64:]
    c64_ref[...] = jnp.concatenate([c_lo, c_lo], axis=1)
    s64_ref[...] = jnp.concatenate([s_lo, s_lo], axis=1) * sgn64_ref[...]
    c128_ref[...] = jnp.concatenate([c_hi, c_hi], axis=1)
    s128_ref[...] = jnp.concatenate([s_hi, s_hi], axis=1) * sgn128_ref[...]


def _rope_tables(positions):
    n = positions.size
    tm = min(TABLE_TILE, n)
    half64, half128 = ATT_HEAD_DIM // 2, RET_QK_DIM // 2
    f64 = ROPE_THETA ** (-(jnp.arange(half64, dtype=F32) * 2.0 / ATT_HEAD_DIM))
    f128 = ROPE_THETA ** (-(jnp.arange(half128, dtype=F32) * 2.0 / RET_QK_DIM))
    invf = jnp.concatenate([f64, f64, f128])[None, :]
    sgn64 = jnp.tile(jnp.concatenate([-jnp.ones(half64, F32), jnp.ones(half64, F32)]), 2)[None, :]
    sgn128 = jnp.concatenate([-jnp.ones(half128, F32), jnp.ones(half128, F32)])[None, :]
    row = pl.BlockSpec((tm, LANES), lambda i: (i, 0))
    const = pl.BlockSpec((1, LANES), lambda i: (0, 0))
    table = jax.ShapeDtypeStruct((n, LANES), F32)
    return pl.pallas_call(
        _rope_table_kernel,
        grid=(n // tm,),
        in_specs=[pl.BlockSpec((tm, 1), lambda i: (i, 0)), const, const, const],
        out_specs=[row, row, row, row],
        out_shape=[table, table, table, table],
        compiler_params=_params("parallel"),
        name="rope_tables",
    )(positions.reshape(n, 1), invf, sgn64, sgn128)


_C_AQ = 0
_C_AK = _C_AQ + ATT_W
_C_AV = _C_AK + ATT_W
_C_IQ = _C_AV + ATT_W
_C_IK = _C_IQ + IDX_QW
_C_IW = _C_IK + LANES
_C_RQ = _C_IW + LANES
_C_RK = _C_RQ + RET_QKW
_C_RV = _C_RK + RET_QKW
_C_RG = _C_RV + RET_VW
_C_GA = _C_RG + RET_VW


def _pack_w_in(w_in, d_model):
    splits = (ATT_W, ATT_W, ATT_W, IDX_QW, IDX_HEAD_DIM, IDX_HEADS,
              RET_QKW, RET_QKW, RET_VW, RET_VW, d_model, d_model)
    cuts = np.cumsum(splits)[:-1].tolist()
    aq, ak, av, iq, ik, iw, rq, rk, rv, rg, ga, gb = jnp.split(w_in, cuts, axis=1)
    iw_pad = jnp.pad(iw, ((0, 0), (0, LANES - IDX_HEADS)))
    packed = jnp.concatenate([aq, ak, av, iq, ik, ik, iw_pad, rq, rk, rv, rg, ga, gb], axis=1)
    return packed.astype(BF16)


def _in_proj_kernel(x_ref, g_ref, w_ref, c64_ref, s64_ref, c128_ref, s128_ref,
                    aq_ref, ak_ref, avt_ref, iq_ref, ik_ref, iw_ref,
                    rq_ref, rk_ref, rv_ref, rgs_ref, sa_ref, sb_ref, *, d_model):
    tm = x_ref.shape[0]
    h = _rms(x_ref[...], g_ref[...]).astype(BF16)

    def proj(c0, width):
        return jnp.dot(h, w_ref[:, c0:c0 + width], preferred_element_type=F32)

    lane = lax.broadcasted_iota(jnp.int32, (tm, LANES), 1)
    first_half = (lane % ATT_HEAD_DIM) < (ATT_HEAD_DIM // 2)
    c64, s64 = c64_ref[...], s64_ref[...]
    c128, s128 = c128_ref[...], s128_ref[...]

    def rot64(y):
        out = []
        for j in range(y.shape[1] // LANES):
            yj = y[:, j * LANES:(j + 1) * LANES]
            partner = jnp.where(first_half, pltpu.roll(yj, LANES - 32, 1), pltpu.roll(yj, 32, 1))
            out.append(yj * c64 + partner * s64)
        return out[0] if len(out) == 1 else jnp.concatenate(out, axis=1)

    def rot128(y):
        out = []
        for j in range(y.shape[1] // LANES):
            yj = y[:, j * LANES:(j + 1) * LANES]
            out.append(yj * c128 + pltpu.roll(yj, 64, 1) * s128)
        return jnp.concatenate(out, axis=1)

    aq_ref[0] = (rot64(proj(_C_AQ, ATT_W)) * (ATT_HEAD_DIM ** -0.5)).astype(BF16)
    ak_ref[0] = rot64(proj(_C_AK, ATT_W)).astype(BF16)
    avt_ref[0] = proj(_C_AV, ATT_W).T.astype(BF16)
    iq_ref[0] = rot64(proj(_C_IQ, IDX_QW)).astype(BF16)
    ik_ref[0] = rot64(proj(_C_IK, LANES)).astype(BF16)
    iw_ref[0] = proj(_C_IW, LANES) * (IDX_HEADS ** -0.5 * IDX_HEAD_DIM ** -0.5)
    rq_ref[0] = rot128(proj(_C_RQ, RET_QKW)).astype(BF16)
    rk_ref[0] = (rot128(proj(_C_RK, RET_QKW)) * (RET_QK_DIM ** -0.5)).astype(BF16)
    rv_ref[0] = proj(_C_RV, RET_VW).astype(BF16)
    rg = proj(_C_RG, RET_VW)
    rgs_ref[0] = rg * jax.nn.sigmoid(rg)
    sa_ref[0] = jax.nn.sigmoid(proj(_C_GA, d_model))
    sb_ref[0] = jax.nn.sigmoid(proj(_C_GA + d_model, d_model))


def _in_proj(x, gain, w_packed, tables):
    b, t, d = x.shape
    tm = min(ROW_TILE, t)
    nt = t // tm
    width = w_packed.shape[1]
    tok = lambda w: pl.BlockSpec((1, tm, w), lambda bi, ti: (bi, ti, 0))
    tab = pl.BlockSpec((tm, LANES), lambda bi, ti: (bi * nt + ti, 0))
    sds = lambda w, dt: jax.ShapeDtypeStruct((b, t, w), dt)
    outs = [
        (tok(ATT_W), sds(ATT_W, BF16)),
        (tok(ATT_W), sds(ATT_W, BF16)),
        (pl.BlockSpec((1, ATT_W, tm), lambda bi, ti: (bi, 0, ti)),
         jax.ShapeDtypeStruct((b, ATT_W, t), BF16)),
        (tok(IDX_QW), sds(IDX_QW, BF16)),
        (tok(LANES), sds(LANES, BF16)),
        (tok(LANES), sds(LANES, F32)),
        (tok(RET_QKW), sds(RET_QKW, BF16)),
        (tok(RET_QKW), sds(RET_QKW, BF16)),
        (tok(RET_VW), sds(RET_VW, BF16)),
        (tok(RET_VW), sds(RET_VW, F32)),
        (tok(d), sds(d, F32)),
        (tok(d), sds(d, F32)),
    ]
    return pl.pallas_call(
        functools.partial(_in_proj_kernel, d_model=d),
        grid=(b, nt),
        in_specs=[pl.BlockSpec((None, tm, d), lambda bi, ti: (bi, ti, 0)),
                  _resident((1, d)), _resident((d, width)), tab, tab, tab, tab],
        out_specs=[o[0] for o in outs],
        out_shape=[o[1] for o in outs],
        compiler_params=_params("parallel", "parallel"),
        name="in_proj",
    )(x, gain, w_packed, *tables)


def _sparse_attn_kernel(iq_ref, iw_ref, aq_ref, ik_ref, ak_ref, avt_ref, out_ref,
                        s_ref, qi_ref, qa_ref, o_ref, m_ref, l_ref, *, topk, seq_len):
    tq, tk = ATT_TQ, ATT_TK
    qi = pl.program_id(1)
    nk = qi + 1

    def krows(kc):
        return pl.ds(pl.multiple_of(kc * tk, tk), tk)

    lane = lax.broadcasted_iota(jnp.int32, (tq, LANES), 1)
    for h in range(ATT_HEADS):
        pair, odd = divmod(h, 2)
        keep = (lane >= ATT_HEAD_DIM) if odd else (lane < ATT_HEAD_DIM)
        cols = slice(pair * LANES, (pair + 1) * LANES)
        qi_ref[h] = jnp.where(keep, iq_ref[0, :, cols], jnp.zeros((tq, LANES), BF16))
        qa_ref[h] = jnp.where(keep, aq_ref[0, :, cols], jnp.zeros((tq, LANES), BF16))
    w_t = iw_ref[0].T

    def score_body(kc, carry):
        k = ik_ref[0, krows(kc), :]
        acc = None
        for h in range(IDX_HEADS):
            dots = lax.dot_general(k, qi_ref[h], (((1,), (1,)), ((), ())), preferred_element_type=F32)
            term = w_t[h:h + 1, :] * jnp.maximum(dots, 0.0)
            acc = term if acc is None else acc + term
        s_ref[krows(kc), :] = acc
        return carry

    lax.fori_loop(0, nk, score_body, 0)

    key_in = lax.broadcasted_iota(jnp.int32, (tk, tq), 0)
    qry_in = lax.broadcasted_iota(jnp.int32, (tk, tq), 1)
    s_ref[krows(qi), :] = jnp.where(key_in <= qry_in, s_ref[krows(qi), :], NEG_INF)

    def count(hits):
        def body(kc, acc):
            hit = hits(s_ref[krows(kc), :], kc)
            return acc + hit.reshape(tk // CNT_ROWS, CNT_ROWS, tq).sum(axis=0)
        acc = lax.fori_loop(0, nk, body, jnp.zeros((CNT_ROWS, tq), jnp.int32))
        return acc.sum(axis=0, keepdims=True)

    def key_to_float(key):
        bits = jnp.where(key >= 0, key, key ^ jnp.int32(0x7FFFFFFF))
        return lax.bitcast_convert_type(bits, F32)

    def count_ge(cand_f):
        return count(lambda s, kc: jnp.where(s >= cand_f, 1, 0))

    def all_set(flags):
        return jnp.min(flags)

    def flag(cond):
        return jnp.where(cond, 1.0, 0.0)

    tot = count_ge(jnp.zeros((1, tq), F32))
    ok = tot >= topk
    cur = jnp.where(ok, 0, INT32_MIN).astype(jnp.int32)
    cnt_cur = jnp.where(ok, tot, 0)
    above_zero = count_ge(jnp.full((1, tq), F32_TINY, F32))
    n_keys = qi * tq + lax.broadcasted_iota(jnp.int32, (1, tq), 1) + 1
    settled = jnp.maximum(flag(ok) * flag(above_zero < topk), flag(n_keys < topk))

    def all_done(cnt_cur):
        return all_set(jnp.maximum(settled, flag(cnt_cur == topk)))

    def search_cond(carry):
        p, _, _, done = carry
        return (p < 32) & (done < 0.5)

    def search_body(carry):
        p, cur, cnt_cur, _ = carry
        for _ in range(SEARCH_UNROLL):
            bit = lax.shift_right_logical(jnp.int32(1 << 30), p - 1)
            cand = cur + bit
            tot = count_ge(key_to_float(cand))
            ok = tot >= topk
            cur = jnp.where(ok, cand, cur)
            cnt_cur = jnp.where(ok, tot, cnt_cur)
            p = p + 1
        return p, cur, cnt_cur, all_done(cnt_cur)

    _, cur, cnt_cur, _ = lax.while_loop(
        search_cond, search_body, (jnp.int32(1), cur, cnt_cur, all_done(cnt_cur)))
    found = cur != INT32_MIN
    tau = jnp.where(found, key_to_float(cur), F32_LOWEST)

    tied = flag(found) * flag(cnt_cur > topk)

    @pl.when(jnp.max(tied) > 0.5)
    def _():
        need = topk - count(lambda s, kc: jnp.where(s > tau, 1, 0))

        def key_index(kc):
            return kc * tk + lax.broadcasted_iota(jnp.int32, (tk, tq), 0)

        def idx_body(p, bound):
            cand = bound + lax.shift_left(jnp.int32(1), seq_len.bit_length() - 1 - p)
            below = count(lambda s, kc: jnp.where(s == tau, jnp.where(key_index(kc) < cand, 1, 0), 0))
            return jnp.where(below < need, cand, bound)

        bound = lax.fori_loop(0, seq_len.bit_length(), idx_body, jnp.zeros((1, tq), jnp.int32))

        def drop_body(kc, carry):
            s = s_ref[krows(kc), :]
            s_ref[krows(kc), :] = jnp.where(s == tau, jnp.where(key_index(kc) > bound, NEG_INF, s), s)
            return carry

        lax.fori_loop(0, nk, drop_body, 0)

    def bias_body(kc, carry):
        s_ref[krows(kc), :] = jnp.where(s_ref[krows(kc), :] >= tau, 0.0, NEG_INF)
        return carry

    lax.fori_loop(0, nk, bias_body, 0)

    m_ref[...] = jnp.full(m_ref.shape, NEG_INF, F32)
    l_ref[...] = jnp.zeros(l_ref.shape, F32)
    o_ref[...] = jnp.zeros(o_ref.shape, F32)

    def attn_body(kc, carry):
        bias = s_ref[krows(kc), :]
        logits = []
        for h in range(ATT_HEADS):
            pair = h // 2
            k = ak_ref[0, krows(kc), pair * LANES:(pair + 1) * LANES]
            logits.append(lax.dot_general(k, qa_ref[h], (((1,), (1,)), ((), ())),
                                          preferred_element_type=F32))
        probs, alphas = [], []
        for h in range(ATT_HEADS):
            lg = logits[h] + bias
            m_old = m_ref[h:h + 1, :]
            m_new = jnp.maximum(m_old, lg.max(axis=0, keepdims=True))
            m_safe = jnp.where(m_new == NEG_INF, 0.0, m_new)
            p = jnp.exp(lg - m_safe)
            alpha = jnp.exp(m_old - m_safe)
            m_ref[h:h + 1, :] = m_new
            l_ref[h:h + 1, :] = alpha * l_ref[h:h + 1, :] + p.sum(axis=0, keepdims=True)
            probs.append(p.astype(BF16))
            alphas.append(alpha)
        for h in range(ATT_HEADS):
            hrows = slice(h * ATT_HEAD_DIM, (h + 1) * ATT_HEAD_DIM)
            v_t = avt_ref[0, hrows, krows(kc)]
            o_ref[hrows, :] = alphas[h] * o_ref[hrows, :] + jnp.dot(v_t, probs[h], preferred_element_type=F32)
        return carry

    lax.fori_loop(0, nk, attn_body, 0)
    for h in range(ATT_HEADS):
        hrows = slice(h * ATT_HEAD_DIM, (h + 1) * ATT_HEAD_DIM)
        o_ref[hrows, :] = o_ref[hrows, :] / l_ref[h:h + 1, :]
    out_ref[0] = o_ref[...].T.astype(BF16)


def _sparse_attention(aq, ak, avt, iq, ik2, iw):
    b, t, _ = aq.shape
    tq = ATT_TQ
    assert t % tq == 0 and ATT_TQ == ATT_TK
    topk = min(TOPK_MAX, t // 4)
    qblk = lambda w: pl.BlockSpec((1, tq, w), lambda bi, qi: (bi, qi, 0))
    full = lambda r, c: pl.BlockSpec((1, r, c), lambda bi, qi: (bi, 0, 0))
    return pl.pallas_call(
        functools.partial(_sparse_attn_kernel, topk=topk, seq_len=t),
        grid=(b, t // tq),
        in_specs=[qblk(IDX_QW), qblk(LANES), qblk(ATT_W),
                  full(t, LANES), full(t, ATT_W), full(ATT_W, t)],
        out_specs=qblk(ATT_W),
        out_shape=jax.ShapeDtypeStruct((b, t, ATT_W), BF16),
        scratch_shapes=[
            pltpu.VMEM((t, tq), F32),
            pltpu.VMEM((IDX_HEADS, tq, LANES), BF16),
            pltpu.VMEM((ATT_HEADS, tq, LANES), BF16),
            pltpu.VMEM((ATT_W, tq), F32),
            pltpu.VMEM((ATT_HEADS, tq), F32),
            pltpu.VMEM((ATT_HEADS, tq), F32),
        ],
        compiler_params=_params("parallel", "parallel"),
        name="sparse_attn",
    )(iq, iw, aq, ik2, ak, avt)


def _retention_kernel(rq_ref, rk_ref, rv_ref, rgs_ref, gain_ref, decay_ref, xi_ref, zeta_ref, cd_ref,
                      out_ref, state_ref):
    c = RET_CHUNK

    @pl.when(pl.program_id(1) == 0)
    def _():
        state_ref[...] = jnp.zeros_like(state_ref)

    def chunk_body(ci, carry):
        rows = pl.ds(pl.multiple_of(ci * c, c), c)
        for h in range(RET_HEADS):
            qk_cols = slice(h * RET_QK_DIM, (h + 1) * RET_QK_DIM)
            v_cols = slice(h * RET_V_DIM, (h + 1) * RET_V_DIM)
            q = rq_ref[0, rows, qk_cols]
            k = rk_ref[0, rows, qk_cols]
            v = rv_ref[0, rows, v_cols]
            state = state_ref[h]
            att = lax.dot_general(q, k, (((1,), (1,)), ((), ())), preferred_element_type=F32)
            att = att * decay_ref[h]
            inner = jnp.dot(att.astype(BF16), v, preferred_element_type=F32)
            cross = jnp.dot(q, state.astype(BF16), preferred_element_type=F32) * xi_ref[h]
            kz_t = (k.astype(F32) * zeta_ref[h]).T.astype(BF16)
            state_ref[h] = state * cd_ref[h, 0:1, :] + jnp.dot(kz_t, v, preferred_element_type=F32)
            o = inner + cross
            mu = jnp.mean(o, axis=-1, keepdims=True)
            dev = o - mu
            var = jnp.mean(dev * dev, axis=-1, keepdims=True)
            y = dev * lax.rsqrt(var + EPS) * gain_ref[:, v_cols]
            out_ref[0, rows, v_cols] = (rgs_ref[0, rows, v_cols] * y).astype(BF16)
        return carry

    lax.fori_loop(0, rq_ref.shape[1] // c, chunk_body, 0)


def _retention(rq, rk, rv, rgs, gn_gain):
    b, t, _ = rq.shape
    rb = min(RET_ROWS, t)
    c = RET_CHUNK
    log_gamma = jnp.log(1.0 - 2.0 ** (-5.0 - jnp.arange(RET_HEADS, dtype=F32)))
    i = jnp.arange(c, dtype=F32)
    diff = i[:, None] - i[None, :]
    decay = jnp.where(diff[None] >= 0,
                      jnp.exp(jnp.maximum(diff, 0.0)[None] * log_gamma[:, None, None]), 0.0)
    xi = jnp.exp((i + 1.0)[None, :] * log_gamma[:, None])
    zeta = jnp.exp((c - 1.0 - i)[None, :] * log_gamma[:, None])
    xi_b = jnp.broadcast_to(xi[:, :, None], (RET_HEADS, c, RET_V_DIM))
    zeta_b = jnp.broadcast_to(zeta[:, :, None], (RET_HEADS, c, RET_QK_DIM))
    cd_b = jnp.broadcast_to(jnp.exp(c * log_gamma)[:, None, None], (RET_HEADS, SUBLANES, RET_V_DIM))
    tok = lambda w: pl.BlockSpec((1, rb, w), lambda bi, ti: (bi, ti, 0))
    return pl.pallas_call(
        _retention_kernel,
        grid=(b, t // rb),
        in_specs=[tok(RET_QKW), tok(RET_QKW), tok(RET_VW), tok(RET_VW), _resident((1, RET_VW)),
                  _resident(decay.shape), _resident(xi_b.shape), _resident(zeta_b.shape),
                  _resident(cd_b.shape)],
        out_specs=tok(RET_VW),
        out_shape=jax.ShapeDtypeStruct((b, t, RET_VW), BF16),
        scratch_shapes=[pltpu.VMEM((RET_HEADS, RET_QK_DIM, RET_V_DIM), F32)],
        compiler_params=_params("parallel", "arbitrary"),
        name="retention",
    )(rq, rk, rv, rgs, gn_gain, decay, xi_b, zeta_b, cd_b)


def _mix_out_kernel(attn_ref, gret_ref, sa_ref, sb_ref, x_ref, wa_ref, wr_ref, wo_ref,
                    g_post_ref, g_pre_ref, x1_ref, h2_ref):
    y_a = jnp.dot(attn_ref[...], wa_ref[...], preferred_element_type=F32)
    y_b = jnp.dot(gret_ref[...], wr_ref[...], preferred_element_type=F32)
    merged = sa_ref[...] * y_a + sb_ref[...] * y_b
    m = jnp.dot(merged.astype(BF16), wo_ref[...], preferred_element_type=F32)
    x1 = x_ref[...] + _rms(m, g_post_ref[...])
    x1_ref[...] = x1
    h2_ref[...] = _rms(x1, g_pre_ref[...]).astype(BF16)


def _mix_out(attn, gret, sa, sb, x, w_br_attn, w_br_ret, w_out, g_post, g_pre_ffn):
    b, t, d = x.shape
    tm = min(ROW_TILE, t)
    tok = lambda w: pl.BlockSpec((None, tm, w), lambda bi, ti: (bi, ti, 0))
    return pl.pallas_call(
        _mix_out_kernel,
        grid=(b, t // tm),
        in_specs=[tok(ATT_W), tok(RET_VW), tok(d), tok(d), tok(d),
                  _resident(w_br_attn.shape), _resident(w_br_ret.shape), _resident(w_out.shape),
                  _resident((1, d)), _resident((1, d))],
        out_specs=[tok(d), tok(d)],
        out_shape=[jax.ShapeDtypeStruct((b, t, d), F32), jax.ShapeDtypeStruct((b, t, d), BF16)],
        compiler_params=_params("parallel", "parallel"),
        name="mix_out",
    )(attn, gret, sa, sb, x, w_br_attn, w_br_ret, w_out, g_post, g_pre_ffn)


def _conv_ffn_kernel(h_ref, x1_ref, wup_ref, cw_ref, cb_ref, wdn_ref, g_ref, out_ref, carry_ref, *, d_ff):
    tm = h_ref.shape[0]

    @pl.when(pl.program_id(1) == 0)
    def _():
        carry_ref[...] = jnp.zeros_like(carry_ref)

    h = h_ref[...]
    row = lax.broadcasted_iota(jnp.int32, (tm, FFN_CHUNK), 0)

    def conv(c0):
        cols = slice(c0, c0 + FFN_CHUNK)
        u = jnp.dot(h, wup_ref[:, cols], preferred_element_type=F32)
        prev = carry_ref[:, cols]
        u1 = jnp.where(row == 0, prev[SUBLANES - 1:SUBLANES, :], pltpu.roll(u, 1, 0))
        u2 = jnp.where(row == 0, prev[SUBLANES - 2:SUBLANES - 1, :],
                       jnp.where(row == 1, prev[SUBLANES - 1:SUBLANES, :], pltpu.roll(u, 2, 0)))
        carry_ref[:, cols] = u[tm - SUBLANES:, :]
        return (cw_ref[0:1, cols] * u2 + cw_ref[1:2, cols] * u1 + cw_ref[2:3, cols] * u
                + cb_ref[:, cols])

    acc = jnp.zeros((tm, out_ref.shape[1]), F32)
    for c0 in range(0, d_ff, FFN_CHUNK):
        gate = conv(c0)
        up = conv(d_ff + c0)
        act = (gate * jax.nn.sigmoid(gate) * up).astype(BF16)
        acc = acc + jnp.dot(act, wdn_ref[c0:c0 + FFN_CHUNK, :], preferred_element_type=F32)
    out_ref[...] = x1_ref[...] + _rms(acc, g_ref[...])


def _conv_ffn(h2, x1, w_up, conv_w, conv_b, w_down, g_post):
    b, t, d = x1.shape
    d_ff = w_down.shape[0]
    assert d_ff % FFN_CHUNK == 0 and conv_w.shape[0] == CONV_WIDTH
    tm = min(ROW_TILE, t)
    tok = pl.BlockSpec((None, tm, d), lambda bi, ti: (bi, ti, 0))
    return pl.pallas_call(
        functools.partial(_conv_ffn_kernel, d_ff=d_ff),
        grid=(b, t // tm),
        in_specs=[tok, tok, _resident(w_up.shape), _resident(conv_w.shape), _resident((1, 2 * d_ff)),
                  _resident(w_down.shape), _resident((1, d))],
        out_specs=tok,
        out_shape=jax.ShapeDtypeStruct((b, t, d), F32),
        scratch_shapes=[pltpu.VMEM((SUBLANES, 2 * d_ff), F32)],
        compiler_params=_params("parallel", "arbitrary"),
        name="conv_ffn",
    )(h2, x1, w_up, conv_w, conv_b, w_down, g_post)


def _layer(x, positions, norm_pre_mix, w_in, w_br_attn, w_br_ret, ret_gn_gain, w_out,
           norm_post_mix, norm_pre_ffn, w_ffn_up, conv_w, conv_b, w_ffn_down, norm_post_ffn):
    d = x.shape[-1]
    row = lambda v: v.reshape(1, -1)
    tables = _rope_tables(positions)
    (aq, ak, avt, iq, ik2, iw, rq, rk, rv, rgs, sa, sb) = _in_proj(
        x, row(norm_pre_mix), _pack_w_in(w_in, d), tables)
    attn = _sparse_attention(aq, ak, avt, iq, ik2, iw)
    gret = _retention(rq, rk, rv, rgs, row(ret_gn_gain))
    x1, h2 = _mix_out(attn, gret, sa, sb, x, w_br_attn.astype(BF16), w_br_ret.astype(BF16),
                      w_out.astype(BF16), row(norm_post_mix), row(norm_pre_ffn))
    return _conv_ffn(h2, x1, w_ffn_up.astype(BF16), conv_w, row(conv_b), w_ffn_down.astype(BF16),
                     row(norm_post_ffn))


def kernel(x, positions, norm_pre_mix, w_in, w_br_attn, w_br_ret, ret_gn_gain, w_out, norm_post_mix,
           norm_pre_ffn, w_ffn_up, conv_w, conv_b, w_ffn_down, norm_post_ffn):
    for l in range(w_in.shape[0]):
        x = _layer(x, positions, norm_pre_mix[l], w_in[l], w_br_attn[l], w_br_ret[l], ret_gn_gain[l],
                   w_out[l], norm_post_mix[l], norm_pre_ffn[l], w_ffn_up[l], conv_w[l], conv_b[l],
                   w_ffn_down[l], norm_post_ffn[l])
    return x
```

```python
import functools

import jax
import jax.numpy as jnp
import numpy as np
from jax import lax
from jax.experimental import pallas as pl
from jax.experimental.pallas import tpu as pltpu

EPS = 1e-6
ROPE_THETA = 10000.0
ATT_HEADS = 8
ATT_HEAD_DIM = 64
IDX_HEADS = 8
IDX_HEAD_DIM = 64
TOPK_MAX = 256
RET_HEADS = 4
RET_QK_DIM = 128
RET_V_DIM = 256
RET_CHUNK = 128
CONV_WIDTH = 3

ATT_W = ATT_HEADS * ATT_HEAD_DIM
IDX_QW = IDX_HEADS * IDX_HEAD_DIM
RET_QKW = RET_HEADS * RET_QK_DIM
RET_VW = RET_HEADS * RET_V_DIM

LANES = 128
SUBLANES = 8
VMEM_LIMIT = 56 * 1024 * 1024

ROW_TILE = 512
TABLE_TILE = 1024
ATT_TQ = 256
ATT_TK = 256
CNT_ROWS = 32
SEARCH_UNROLL = 4
SEARCH_MAX_PASSES = 288
RET_ROWS = 512
FFN_CHUNK = 512

F32 = jnp.float32
BF16 = jnp.bfloat16
NEG_INF = float("-inf")
F32_LOWEST = float(np.finfo(np.float32).min)
F32_TINY = float(np.finfo(np.float32).tiny)
LOG2_E = float(np.log2(np.e))


def _params(*semantics):
    return pltpu.CompilerParams(dimension_semantics=semantics, vmem_limit_bytes=VMEM_LIMIT)


def _resident(shape):
    zeros = (0,) * len(shape)
    return pl.BlockSpec(shape, lambda *_: zeros, pipeline_mode=pl.Buffered(1))


def _rms(x, gain):
    return x * lax.rsqrt(jnp.mean(x * x, axis=-1, keepdims=True) + EPS) * gain


def _rope_table_kernel(pos_ref, invf_ref, sgn64_ref, sgn128_ref, c64_ref, s64_ref, c128_ref, s128_ref):
    ang = pos_ref[...].astype(F32) * invf_ref[...]
    c = jnp.cos(ang)
    s = jnp.sin(ang)
    c_lo, c_hi = c[:, :64], c[:, 64:]
    s_lo, s_hi = s[:, :64], s[:, 64:]
    c64_ref[...] = jnp.concatenate([c_lo, c_lo], axis=1)
    s64_ref[...] = jnp.concatenate([s_lo, s_lo], axis=1) * sgn64_ref[...]
    c128_ref[...] = jnp.concatenate([c_hi, c_hi], axis=1)
    s128_ref[...] = jnp.concatenate([s_hi, s_hi], axis=1) * sgn128_ref[...]


def _rope_tables(positions):
    n = positions.size
    tm = min(TABLE_TILE, n)
    half64, half128 = ATT_HEAD_DIM // 2, RET_QK_DIM // 2
    f64 = ROPE_THETA ** (-(jnp.arange(half64, dtype=F32) * 2.0 / ATT_HEAD_DIM))
    f128 = ROPE_THETA ** (-(jnp.arange(half128, dtype=F32) * 2.0 / RET_QK_DIM))
    invf = jnp.concatenate([f64, f64, f128])[None, :]
    sgn64 = jnp.tile(jnp.concatenate([-jnp.ones(half64, F32), jnp.ones(half64, F32)]), 2)[None, :]
    sgn128 = jnp.concatenate([-jnp.ones(half128, F32), jnp.ones(half128, F32)])[None, :]
    row = pl.BlockSpec((tm, LANES), lambda i: (i, 0))
    const = pl.BlockSpec((1, LANES), lambda i: (0, 0))
    table = jax.ShapeDtypeStruct((n, LANES), F32)
    return pl.pallas_call(
        _rope_table_kernel,
        grid=(n // tm,),
        in_specs=[pl.BlockSpec((tm, 1), lambda i: (i, 0)), const, const, const],
        out_specs=[row, row, row, row],
        out_shape=[table, table, table, table],
        compiler_params=_params("parallel"),
        name="rope_tables",
    )(positions.reshape(n, 1), invf, sgn64, sgn128)


_C_AQ = 0
_C_AK = _C_AQ + ATT_W
_C_AV = _C_AK + ATT_W
_C_IQ = _C_AV + ATT_W
_C_IK = _C_IQ + IDX_QW
_C_IW = _C_IK + LANES
_C_RQ = _C_IW + LANES
_C_RK = _C_RQ + RET_QKW
_C_RV = _C_RK + RET_QKW
_C_RG = _C_RV + RET_VW
_C_GA = _C_RG + RET_VW


def _pack_w_in(w_in, d_model):
    splits = (ATT_W, ATT_W, ATT_W, IDX_QW, IDX_HEAD_DIM, IDX_HEADS,
              RET_QKW, RET_QKW, RET_VW, RET_VW, d_model, d_model)
    cuts = np.cumsum(splits)[:-1].tolist()
    aq, ak, av, iq, ik, iw, rq, rk, rv, rg, ga, gb = jnp.split(w_in, cuts, axis=1)
    iw_pad = jnp.pad(iw, ((0, 0), (0, LANES - IDX_HEADS)))
    packed = jnp.concatenate([aq, ak, av, iq, ik, ik, iw_pad, rq, rk, rv, rg, ga, gb], axis=1)
    return packed.astype(BF16)


def _in_proj_kernel(x_ref, g_ref, w_ref, c64_ref, s64_ref, c128_ref, s128_ref,
                    aq_ref, ak_ref, avt_ref, iq_ref, ik_ref, iw_ref,
                    rq_ref, rk_ref, rv_ref, rgs_ref, sa_ref, sb_ref, *, d_model):
    tm = x_ref.shape[0]
    h = _rms(x_ref[...], g_ref[...]).astype(BF16)

    def proj(c0, width):
        return jnp.dot(h, w_ref[:, c0:c0 + width], preferred_element_type=F32)

    lane = lax.broadcasted_iota(jnp.int32, (tm, LANES), 1)
    first_half = (lane % ATT_HEAD_DIM) < (ATT_HEAD_DIM // 2)
    c64, s64 = c64_ref[...], s64_ref[...]
    c128, s128 = c128_ref[...], s128_ref[...]

    def rot64(y):
        out = []
        for j in range(y.shape[1] // LANES):
            yj = y[:, j * LANES:(j + 1) * LANES]
            partner = jnp.where(first_half, pltpu.roll(yj, LANES - 32, 1), pltpu.roll(yj, 32, 1))
            out.append(yj * c64 + partner * s64)
        return out[0] if len(out) == 1 else jnp.concatenate(out, axis=1)

    def rot128(y):
        out = []
        for j in range(y.shape[1] // LANES):
            yj = y[:, j * LANES:(j + 1) * LANES]
            out.append(yj * c128 + pltpu.roll(yj, 64, 1) * s128)
        return jnp.concatenate(out, axis=1)

    aq_ref[0] = (rot64(proj(_C_AQ, ATT_W)) * (ATT_HEAD_DIM ** -0.5 * LOG2_E)).astype(BF16)
    ak_ref[0] = rot64(proj(_C_AK, ATT_W)).astype(BF16)
    avt_ref[0] = proj(_C_AV, ATT_W).T.astype(BF16)
    iq_ref[0] = rot64(proj(_C_IQ, IDX_QW)).astype(BF16)
    ik_ref[0] = rot64(proj(_C_IK, LANES)).astype(BF16)
    iw_ref[0] = proj(_C_IW, LANES) * (IDX_HEADS ** -0.5 * IDX_HEAD_DIM ** -0.5)
    rq_ref[0] = rot128(proj(_C_RQ, RET_QKW)).astype(BF16)
    rk_ref[0] = (rot128(proj(_C_RK, RET_QKW)) * (RET_QK_DIM ** -0.5)).astype(BF16)
    rv_ref[0] = proj(_C_RV, RET_VW).astype(BF16)
    rg = proj(_C_RG, RET_VW)
    rgs_ref[0] = rg * jax.nn.sigmoid(rg)
    sa_ref[0] = jax.nn.sigmoid(proj(_C_GA, d_model))
    sb_ref[0] = jax.nn.sigmoid(proj(_C_GA + d_model, d_model))


def _in_proj(x, gain, w_packed, tables):
    b, t, d = x.shape
    tm = min(ROW_TILE, t)
    nt = t // tm
    width = w_packed.shape[1]
    tok = lambda w: pl.BlockSpec((1, tm, w), lambda bi, ti: (bi, ti, 0))
    tab = pl.BlockSpec((tm, LANES), lambda bi, ti: (bi * nt + ti, 0))
    sds = lambda w, dt: jax.ShapeDtypeStruct((b, t, w), dt)
    outs = [
        (tok(ATT_W), sds(ATT_W, BF16)),
        (tok(ATT_W), sds(ATT_W, BF16)),
        (pl.BlockSpec((1, ATT_W, tm), lambda bi, ti: (bi, 0, ti)),
         jax.ShapeDtypeStruct((b, ATT_W, t), BF16)),
        (tok(IDX_QW), sds(IDX_QW, BF16)),
        (tok(LANES), sds(LANES, BF16)),
        (tok(LANES), sds(LANES, F32)),
        (tok(RET_QKW), sds(RET_QKW, BF16)),
        (tok(RET_QKW), sds(RET_QKW, BF16)),
        (tok(RET_VW), sds(RET_VW, BF16)),
        (tok(RET_VW), sds(RET_VW, F32)),
        (tok(d), sds(d, F32)),
        (tok(d), sds(d, F32)),
    ]
    return pl.pallas_call(
        functools.partial(_in_proj_kernel, d_model=d),
        grid=(b, nt),
        in_specs=[pl.BlockSpec((None, tm, d), lambda bi, ti: (bi, ti, 0)),
                  _resident((1, d)), _resident((d, width)), tab, tab, tab, tab],
        out_specs=[o[0] for o in outs],
        out_shape=[o[1] for o in outs],
        compiler_params=_params("parallel", "parallel"),
        name="in_proj",
    )(x, gain, w_packed, *tables)


def _sparse_attn_kernel(iq_ref, iw_ref, aq_ref, ik_ref, ak_ref, avt_ref, out_ref,
                        s_ref, qi_ref, qa_ref, o_ref, m_ref, l_ref, pa_ref, pb_ref, *, topk, seq_len):
    tq, tk = ATT_TQ, ATT_TK
    qi = pl.program_id(1)
    nk = qi + 1

    def krows(kc):
        return pl.ds(pl.multiple_of(kc * tk, tk), tk)

    lane = lax.broadcasted_iota(jnp.int32, (tq, LANES), 1)
    for h in range(ATT_HEADS):
        pair, odd = divmod(h, 2)
        keep = (lane >= ATT_HEAD_DIM) if odd else (lane < ATT_HEAD_DIM)
        cols = slice(pair * LANES, (pair + 1) * LANES)
        qi_ref[h] = jnp.where(keep, iq_ref[0, :, cols], jnp.zeros((tq, LANES), BF16))
        qa_ref[h] = jnp.where(keep, aq_ref[0, :, cols], jnp.zeros((tq, LANES), BF16))
    w_t = iw_ref[0].T

    n_trips = (nk + 1) // 2
    last = nk - 1

    def score_dots(kc, dst_ref):
        k = ik_ref[0, krows(kc), :]
        for h in range(IDX_HEADS):
            dst_ref[h] = lax.dot_general(k, qi_ref[h], (((1,), (1,)), ((), ())), preferred_element_type=F32)

    def score_reduce(src_ref, kc, bounds):
        acc = None
        for h in range(IDX_HEADS):
            term = w_t[h:h + 1, :] * jnp.maximum(src_ref[h], 0.0)
            acc = term if acc is None else acc + term
        s_ref[krows(kc), :] = acc
        groups = acc.reshape(tk // SUBLANES, SUBLANES, tq)
        return jnp.maximum(bounds[0], groups.max(axis=0)), jnp.minimum(bounds[1], groups.min(axis=0))

    def score_body(j, bounds):
        c0 = 2 * j
        c1 = jnp.minimum(c0 + 1, last)
        score_dots(c1, pb_ref)
        bounds = score_reduce(pa_ref, c0, bounds)
        score_dots(jnp.minimum(c0 + 2, last), pa_ref)
        return score_reduce(pb_ref, c1, bounds)

    score_dots(0, pa_ref)
    smax, smin = lax.fori_loop(
        0, n_trips, score_body,
        (jnp.full((SUBLANES, tq), NEG_INF, F32), jnp.full((SUBLANES, tq), -NEG_INF, F32)))
    smax = smax.max(axis=0, keepdims=True)
    smin = smin.min(axis=0, keepdims=True)

    key_in = lax.broadcasted_iota(jnp.int32, (tk, tq), 0)
    qry_in = lax.broadcasted_iota(jnp.int32, (tk, tq), 1)
    s_ref[krows(qi), :] = jnp.where(key_in <= qry_in, s_ref[krows(qi), :], NEG_INF)

    def count(hits):
        def body(kc, acc):
            hit = hits(s_ref[krows(kc), :], kc)
            return acc + hit.reshape(tk // CNT_ROWS, CNT_ROWS, tq).sum(axis=0)
        acc = lax.fori_loop(0, nk, body, jnp.zeros((CNT_ROWS, tq), jnp.int32))
        return acc.sum(axis=0, keepdims=True)

    def count_ge(cand_f):
        def body(kc, acc):
            s = s_ref[krows(kc), :]
            for g in range(tk // CNT_ROWS):
                acc = jnp.where(s[g * CNT_ROWS:(g + 1) * CNT_ROWS] >= cand_f, acc + 1, acc)
            return acc
        acc = lax.fori_loop(0, nk, body, jnp.zeros((CNT_ROWS, tq), jnp.int32))
        return acc.sum(axis=0, keepdims=True)

    def flag(cond):
        return jnp.where(cond, 1.0, 0.0)

    n_keys = qi * tq + lax.broadcasted_iota(jnp.int32, (1, tq), 1) + 1
    at_zero = count_ge(jnp.zeros((1, tq), F32))
    above_zero = count_ge(jnp.full((1, tq), F32_TINY, F32))
    positive = above_zero >= topk
    negative = at_zero < topk
    lo = jnp.where(positive, F32_TINY, jnp.where(negative, smin, 0.0))
    cnt_lo = jnp.where(positive, above_zero, jnp.where(negative, n_keys, at_zero))
    hi = jnp.where(positive, 2.0 * smax, jnp.where(negative, -0.0, 0.0))
    few = n_keys < topk
    settled = jnp.maximum(flag(few), (1.0 - flag(positive)) * (1.0 - flag(negative)))

    def search_cond(carry):
        p, _, _, _, done = carry
        return (p < SEARCH_MAX_PASSES) & (done < 0.5)

    def search_body(carry):
        p, lo, hi, cnt_lo, _ = carry
        closed = jnp.zeros((1, tq), F32)
        for _ in range(SEARCH_UNROLL):
            mid = 0.5 * lo + 0.5 * hi
            tot = count_ge(mid)
            ok = tot >= topk
            closed = jnp.maximum(closed, jnp.maximum(flag(mid <= lo), flag(mid >= hi)))
            lo = jnp.where(ok, mid, lo)
            cnt_lo = jnp.where(ok, tot, cnt_lo)
            hi = jnp.where(ok, hi, mid)
        done = jnp.min(jnp.maximum(jnp.maximum(settled, closed), flag(cnt_lo == topk)))
        return p + SEARCH_UNROLL, lo, hi, cnt_lo, done

    done0 = jnp.min(jnp.maximum(settled, flag(cnt_lo == topk)))
    _, lo, _, cnt_lo, _ = lax.while_loop(search_cond, search_body, (jnp.int32(0), lo, hi, cnt_lo, done0))
    tau = jnp.where(few, F32_LOWEST, lo)

    tied = (1.0 - flag(few)) * flag(cnt_lo > topk)

    @pl.when(jnp.max(tied) > 0.5)
    def _():
        need = topk - count(lambda s, kc: jnp.where(s > tau, 1, 0))

        def key_index(kc):
            return kc * tk + lax.broadcasted_iota(jnp.int32, (tk, tq), 0)

        def idx_body(p, bound):
            cand = bound + lax.shift_left(jnp.int32(1), seq_len.bit_length() - 1 - p)
            below = count(lambda s, kc: jnp.where(s == tau, jnp.where(key_index(kc) < cand, 1, 0), 0))
            return jnp.where(below < need, cand, bound)

        bound = lax.fori_loop(0, seq_len.bit_length(), idx_body, jnp.zeros((1, tq), jnp.int32))

        def drop_body(kc, carry):
            s = s_ref[krows(kc), :]
            s_ref[krows(kc), :] = jnp.where(s == tau, jnp.where(key_index(kc) > bound, NEG_INF, s), s)
            return carry

        lax.fori_loop(0, nk, drop_body, 0)

    def bias_body(kc, carry):
        s_ref[krows(kc), :] = jnp.where(s_ref[krows(kc), :] >= tau, 0.0, NEG_INF)
        return carry

    lax.fori_loop(0, nk, bias_body, 0)

    m_ref[...] = jnp.full(m_ref.shape, NEG_INF, F32)
    l_ref[...] = jnp.zeros(l_ref.shape, F32)
    o_ref[...] = jnp.zeros(o_ref.shape, F32)

    def logit_dots(kc, dst_ref):
        for h in range(ATT_HEADS):
            pair = h // 2
            k = ak_ref[0, krows(kc), pair * LANES:(pair + 1) * LANES]
            dst_ref[h] = lax.dot_general(k, qa_ref[h], (((1,), (1,)), ((), ())), preferred_element_type=F32)

    def softmax_pv(src_ref, kc, live):
        bias = s_ref[krows(kc), :]
        if live is not True:
            bias = jnp.where(live, bias, NEG_INF)
        for h in range(ATT_HEADS):
            hrows = slice(h * ATT_HEAD_DIM, (h + 1) * ATT_HEAD_DIM)
            lg = src_ref[h] + bias
            m_old = m_ref[h:h + 1, :]
            m_new = jnp.maximum(m_old, lg.max(axis=0, keepdims=True))
            m_safe = jnp.where(m_new == NEG_INF, 0.0, m_new)
            p = jnp.exp2(lg - m_safe)
            alpha = jnp.exp2(m_old - m_safe)
            m_ref[h:h + 1, :] = m_new
            l_ref[h:h + 1, :] = alpha * l_ref[h:h + 1, :] + p.sum(axis=0, keepdims=True)
            v_t = avt_ref[0, hrows, krows(kc)]
            o_ref[hrows, :] = alpha * o_ref[hrows, :] + jnp.dot(v_t, p.astype(BF16), preferred_element_type=F32)

    def attn_body(j, carry):
        c0 = 2 * j
        c1 = jnp.minimum(c0 + 1, last)
        logit_dots(c1, pb_ref)
        softmax_pv(pa_ref, c0, True)
        logit_dots(jnp.minimum(c0 + 2, last), pa_ref)
        softmax_pv(pb_ref, c1, c0 + 1 <= last)
        return carry

    logit_dots(0, pa_ref)
    lax.fori_loop(0, n_trips, attn_body, 0)
    for h in range(ATT_HEADS):
        hrows = slice(h * ATT_HEAD_DIM, (h + 1) * ATT_HEAD_DIM)
        o_ref[hrows, :] = o_ref[hrows, :] / l_ref[h:h + 1, :]
    out_ref[0] = o_ref[...].T.astype(BF16)


def _sparse_attention(aq, ak, avt, iq, ik2, iw):
    b, t, _ = aq.shape
    tq = ATT_TQ
    assert t % tq == 0 and ATT_TQ == ATT_TK
    topk = min(TOPK_MAX, t // 4)
    qblk = lambda w: pl.BlockSpec((1, tq, w), lambda bi, qi: (bi, qi, 0))
    full = lambda r, c: pl.BlockSpec((1, r, c), lambda bi, qi: (bi, 0, 0))
    return pl.pallas_call(
        functools.partial(_sparse_attn_kernel, topk=topk, seq_len=t),
        grid=(b, t // tq),
        in_specs=[qblk(IDX_QW), qblk(LANES), qblk(ATT_W),
                  full(t, LANES), full(t, ATT_W), full(ATT_W, t)],
        out_specs=qblk(ATT_W),
        out_shape=jax.ShapeDtypeStruct((b, t, ATT_W), BF16),
        scratch_shapes=[
            pltpu.VMEM((t, tq), F32),
            pltpu.VMEM((IDX_HEADS, tq, LANES), BF16),
            pltpu.VMEM((ATT_HEADS, tq, LANES), BF16),
            pltpu.VMEM((ATT_W, tq), F32),
            pltpu.VMEM((ATT_HEADS, tq), F32),
            pltpu.VMEM((ATT_HEADS, tq), F32),
            pltpu.VMEM((ATT_HEADS, ATT_TK, tq), F32),
            pltpu.VMEM((ATT_HEADS, ATT_TK, tq), F32),
        ],
        compiler_params=_params("parallel", "parallel"),
        name="sparse_attn",
    )(iq, iw, aq, ik2, ak, avt)


def _retention_kernel(rq_ref, rk_ref, rv_ref, rgs_ref, gain_ref, decay_ref, xi_ref, zeta_ref, cd_ref,
                      out_ref, state_ref):
    c = RET_CHUNK

    @pl.when(pl.program_id(1) == 0)
    def _():
        state_ref[...] = jnp.zeros_like(state_ref)

    def chunk_body(ci, carry):
        rows = pl.ds(pl.multiple_of(ci * c, c), c)
        heads = range(RET_HEADS)
        qk_cols = [slice(h * RET_QK_DIM, (h + 1) * RET_QK_DIM) for h in heads]
        v_slices = [slice(h * RET_V_DIM, (h + 1) * RET_V_DIM) for h in heads]
        q = [rq_ref[0, rows, qk_cols[h]] for h in heads]
        k = [rk_ref[0, rows, qk_cols[h]] for h in heads]
        v = [rv_ref[0, rows, v_slices[h]] for h in heads]
        att = [lax.dot_general(q[h], k[h], (((1,), (1,)), ((), ())), preferred_element_type=F32)
               for h in heads]
        cross = [jnp.dot(q[h], state_ref[h].astype(BF16), preferred_element_type=F32) * xi_ref[h]
                 for h in heads]
        for h in heads:
            kz_t = (k[h].astype(F32) * zeta_ref[h]).T.astype(BF16)
            state_ref[h] = (state_ref[h] * cd_ref[h, 0:1, :]
                            + jnp.dot(kz_t, v[h], preferred_element_type=F32))
        for h in heads:
            v_cols = v_slices[h]
            inner = jnp.dot((att[h] * decay_ref[h]).astype(BF16), v[h], preferred_element_type=F32)
            o = inner + cross[h]
            mu = jnp.mean(o, axis=-1, keepdims=True)
            dev = o - mu
            var = jnp.mean(dev * dev, axis=-1, keepdims=True)
            y = dev * lax.rsqrt(var + EPS) * gain_ref[:, v_cols]
            out_ref[0, rows, v_cols] = (rgs_ref[0, rows, v_cols] * y).astype(BF16)
        return carry

    lax.fori_loop(0, rq_ref.shape[1] // c, chunk_body, 0)


def _retention(rq, rk, rv, rgs, gn_gain):
    b, t, _ = rq.shape
    rb = min(RET_ROWS, t)
    c = RET_CHUNK
    log_gamma = jnp.log(1.0 - 2.0 ** (-5.0 - jnp.arange(RET_HEADS, dtype=F32)))
    i = jnp.arange(c, dtype=F32)
    diff = i[:, None] - i[None, :]
    decay = jnp.where(diff[None] >= 0,
                      jnp.exp(jnp.maximum(diff, 0.0)[None] * log_gamma[:, None, None]), 0.0)
    xi = jnp.exp((i + 1.0)[None, :] * log_gamma[:, None])
    zeta = jnp.exp((c - 1.0 - i)[None, :] * log_gamma[:, None])
    xi_b = jnp.broadcast_to(xi[:, :, None], (RET_HEADS, c, RET_V_DIM))
    zeta_b = jnp.broadcast_to(zeta[:, :, None], (RET_HEADS, c, RET_QK_DIM))
    cd_b = jnp.broadcast_to(jnp.exp(c * log_gamma)[:, None, None], (RET_HEADS, SUBLANES, RET_V_DIM))
    tok = lambda w: pl.BlockSpec((1, rb, w), lambda bi, ti: (bi, ti, 0))
    return pl.pallas_call(
        _retention_kernel,
        grid=(b, t // rb),
        in_specs=[tok(RET_QKW), tok(RET_QKW), tok(RET_VW), tok(RET_VW), _resident((1, RET_VW)),
                  _resident(decay.shape), _resident(xi_b.shape), _resident(zeta_b.shape),
                  _resident(cd_b.shape)],
        out_specs=tok(RET_VW),
        out_shape=jax.ShapeDtypeStruct((b, t, RET_VW), BF16),
        scratch_shapes=[pltpu.VMEM((RET_HEADS, RET_QK_DIM, RET_V_DIM), F32)],
        compiler_params=_params("parallel", "arbitrary"),
        name="retention",
    )(rq, rk, rv, rgs, gn_gain, decay, xi_b, zeta_b, cd_b)


def _mix_out_kernel(attn_ref, gret_ref, sa_ref, sb_ref, x_ref, wa_ref, wr_ref, wo_ref,
                    g_post_ref, g_pre_ref, x1_ref, h2_ref):
    y_a = jnp.dot(attn_ref[...], wa_ref[...], preferred_element_type=F32)
    y_b = jnp.dot(gret_ref[...], wr_ref[...], preferred_element_type=F32)
    merged = sa_ref[...] * y_a + sb_ref[...] * y_b
    m = jnp.dot(merged.astype(BF16), wo_ref[...], preferred_element_type=F32)
    x1 = x_ref[...] + _rms(m, g_post_ref[...])
    x1_ref[...] = x1
    h2_ref[...] = _rms(x1, g_pre_ref[...]).astype(BF16)


def _mix_out(attn, gret, sa, sb, x, w_br_attn, w_br_ret, w_out, g_post, g_pre_ffn):
    b, t, d = x.shape
    tm = min(ROW_TILE, t)
    tok = lambda w: pl.BlockSpec((None, tm, w), lambda bi, ti: (bi, ti, 0))
    return pl.pallas_call(
        _mix_out_kernel,
        grid=(b, t // tm),
        in_specs=[tok(ATT_W), tok(RET_VW), tok(d), tok(d), tok(d),
                  _resident(w_br_attn.shape), _resident(w_br_ret.shape), _resident(w_out.shape),
                  _resident((1, d)), _resident((1, d))],
        out_specs=[tok(d), tok(d)],
        out_shape=[jax.ShapeDtypeStruct((b, t, d), F32), jax.ShapeDtypeStruct((b, t, d), BF16)],
        compiler_params=_params("parallel", "parallel"),
        name="mix_out",
    )(attn, gret, sa, sb, x, w_br_attn, w_br_ret, w_out, g_post, g_pre_ffn)


def _conv_ffn_kernel(h_ref, x1_ref, wup_ref, cw_ref, cb_ref, wdn_ref, g_ref, out_ref, carry_ref, *, d_ff):
    tm = h_ref.shape[0]

    @pl.when(pl.program_id(1) == 0)
    def _():
        carry_ref[...] = jnp.zeros_like(carry_ref)

    def up_proj(c0, width):
        return (jnp.dot(h_ref[...], wup_ref[:, c0:c0 + width], preferred_element_type=F32),
                jnp.dot(h_ref[...], wup_ref[:, d_ff + c0:d_ff + c0 + width], preferred_element_type=F32))

    def shifted(u, prev, k):
        row = lax.broadcasted_iota(jnp.int32, (SUBLANES, u.shape[1]), 0)
        r = pltpu.roll(u, k, 0)
        head = r[:SUBLANES]
        for j in range(k):
            head = jnp.where(row == j, prev[SUBLANES - k + j:SUBLANES - k + j + 1, :], head)
        return jnp.concatenate([head, r[SUBLANES:]], axis=0)

    def conv(u, c0):
        cols = slice(c0, c0 + u.shape[1])
        prev = carry_ref[:, cols]
        carry_ref[:, cols] = u[tm - SUBLANES:, :]
        return (cw_ref[0:1, cols] * shifted(u, prev, 2) + cw_ref[1:2, cols] * shifted(u, prev, 1)
                + cw_ref[2:3, cols] * u + cb_ref[:, cols])

    chunks = [(c0, min(FFN_CHUNK, d_ff - c0)) for c0 in range(0, d_ff, FFN_CHUNK)]
    acc = jnp.zeros((tm, out_ref.shape[1]), F32)
    pre = up_proj(*chunks[0])
    for i, (c0, width) in enumerate(chunks):
        nxt = up_proj(*chunks[i + 1]) if i + 1 < len(chunks) else None
        gate = conv(pre[0], c0)
        up = conv(pre[1], d_ff + c0)
        act = (gate * jax.nn.sigmoid(gate) * up).astype(BF16)
        acc = acc + jnp.dot(act, wdn_ref[c0:c0 + width, :], preferred_element_type=F32)
        pre = nxt
    out_ref[...] = x1_ref[...] + _rms(acc, g_ref[...])


def _conv_ffn(h2, x1, w_up, conv_w, conv_b, w_down, g_post):
    b, t, d = x1.shape
    d_ff = w_down.shape[0]
    assert d_ff % LANES == 0 and conv_w.shape[0] == CONV_WIDTH
    tm = min(ROW_TILE, t)
    tok = pl.BlockSpec((None, tm, d), lambda bi, ti: (bi, ti, 0))
    return pl.pallas_call(
        functools.partial(_conv_ffn_kernel, d_ff=d_ff),
        grid=(b, t // tm),
        in_specs=[tok, tok, _resident(w_up.shape), _resident(conv_w.shape), _resident((1, 2 * d_ff)),
                  _resident(w_down.shape), _resident((1, d))],
        out_specs=tok,
        out_shape=jax.ShapeDtypeStruct((b, t, d), F32),
        scratch_shapes=[pltpu.VMEM((SUBLANES, 2 * d_ff), F32)],
        compiler_params=_params("parallel", "arbitrary"),
        name="conv_ffn",
    )(h2, x1, w_up, conv_w, conv_b, w_down, g_post)


def _layer(x, positions, norm_pre_mix, w_in, w_br_attn, w_br_ret, ret_gn_gain, w_out,
           norm_post_mix, norm_pre_ffn, w_ffn_up, conv_w, conv_b, w_ffn_down, norm_post_ffn):
    d = x.shape[-1]
    row = lambda v: v.reshape(1, -1)
    tables = _rope_tables(positions)
    (aq, ak, avt, iq, ik2, iw, rq, rk, rv, rgs, sa, sb) = _in_proj(
        x, row(norm_pre_mix), _pack_w_in(w_in, d), tables)
    attn = _sparse_attention(aq, ak, avt, iq, ik2, iw)
    gret = _retention(rq, rk, rv, rgs, row(ret_gn_gain))
    x1, h2 = _mix_out(attn, gret, sa, sb, x, w_br_attn.astype(BF16), w_br_ret.astype(BF16),
                      w_out.astype(BF16), row(norm_post_mix), row(norm_pre_ffn))
    return _conv_ffn(h2, x1, w_ffn_up.astype(BF16), conv_w, row(conv_b), w_ffn_down.astype(BF16),
                     row(norm_post_ffn))


def kernel(x, positions, norm_pre_mix, w_in, w_br_attn, w_br_ret, ret_gn_gain, w_out, norm_post_mix,
           norm_pre_ffn, w_ffn_up, conv_w, conv_b, w_ffn_down, norm_post_ffn):
    for l in range(w_in.shape[0]):
        x = _layer(x, positions, norm_pre_mix[l], w_in[l], w_br_attn[l], w_br_ret[l], ret_gn_gain[l],
                   w_out[l], norm_post_mix[l], norm_pre_ffn[l], w_ffn_up[l], conv_w[l], conv_b[l],
                   w_ffn_down[l], norm_post_ffn[l])
    return x
```

```python
import functools

import jax
import jax.numpy as jnp
import numpy as np
from jax import lax
from jax.experimental import pallas as pl
from jax.experimental.pallas import tpu as pltpu

EPS = 1e-6
ROPE_THETA = 10000.0
ATT_HEADS = 8
ATT_HEAD_DIM = 64
IDX_HEADS = 8
IDX_HEAD_DIM = 64
TOPK_MAX = 256
RET_HEADS = 4
RET_QK_DIM = 128
RET_V_DIM = 256
RET_CHUNK = 128
CONV_WIDTH = 3

ATT_W = ATT_HEADS * ATT_HEAD_DIM
IDX_QW = IDX_HEADS * IDX_HEAD_DIM
RET_QKW = RET_HEADS * RET_QK_DIM
RET_VW = RET_HEADS * RET_V_DIM

LANES = 128
SUBLANES = 8
VMEM_LIMIT = 56 * 1024 * 1024

ROW_TILE = 512
TABLE_TILE = 1024
ATT_TQ = 256
ATT_TK = 256
CNT_ROWS = 32
SEARCH_UNROLL = 4
SEARCH_WARM_TRIPS = 3
SEARCH_MAX_PASSES = 288
RET_ROWS = 512
FFN_CHUNK = 512

F32 = jnp.float32
BF16 = jnp.bfloat16
NEG_INF = float("-inf")
F32_LOWEST = float(np.finfo(np.float32).min)
F32_TINY = float(np.finfo(np.float32).tiny)
LOG2_E = float(np.log2(np.e))


def _params(*semantics):
    return pltpu.CompilerParams(dimension_semantics=semantics, vmem_limit_bytes=VMEM_LIMIT)


def _resident(shape):
    zeros = (0,) * len(shape)
    return pl.BlockSpec(shape, lambda *_: zeros, pipeline_mode=pl.Buffered(1))


def _rms(x, gain):
    return x * lax.rsqrt(jnp.mean(x * x, axis=-1, keepdims=True) + EPS) * gain


def _rope_table_kernel(pos_ref, invf_ref, sgn64_ref, sgn128_ref, c64_ref, s64_ref, c128_ref, s128_ref):
    ang = pos_ref[...].astype(F32) * invf_ref[...]
    c = jnp.cos(ang)
    s = jnp.sin(ang)
    c_lo, c_hi = c[:, :64], c[:, 64:]
    s_lo, s_hi = s[:, :64], s[:, 64:]
    c64_ref[...] = jnp.concatenate([c_lo, c_lo], axis=1)
    s64_ref[...] = jnp.concatenate([s_lo, s_lo], axis=1) * sgn64_ref[...]
    c128_ref[...] = jnp.concatenate([c_hi, c_hi], axis=1)
    s128_ref[...] = jnp.concatenate([s_hi, s_hi], axis=1) * sgn128_ref[...]


def _rope_tables(positions):
    n = positions.size
    tm = min(TABLE_TILE, n)
    half64, half128 = ATT_HEAD_DIM // 2, RET_QK_DIM // 2
    f64 = ROPE_THETA ** (-(jnp.arange(half64, dtype=F32) * 2.0 / ATT_HEAD_DIM))
    f128 = ROPE_THETA ** (-(jnp.arange(half128, dtype=F32) * 2.0 / RET_QK_DIM))
    invf = jnp.concatenate([f64, f64, f128])[None, :]
    sgn64 = jnp.tile(jnp.concatenate([-jnp.ones(half64, F32), jnp.ones(half64, F32)]), 2)[None, :]
    sgn128 = jnp.concatenate([-jnp.ones(half128, F32), jnp.ones(half128, F32)])[None, :]
    row = pl.BlockSpec((tm, LANES), lambda i: (i, 0))
    const = pl.BlockSpec((1, LANES), lambda i: (0, 0))
    table = jax.ShapeDtypeStruct((n, LANES), F32)
    return pl.pallas_call(
        _rope_table_kernel,
        grid=(n // tm,),
        in_specs=[pl.BlockSpec((tm, 1), lambda i: (i, 0)), const, const, const],
        out_specs=[row, row, row, row],
        out_shape=[table, table, table, table],
        compiler_params=_params("parallel"),
        name="rope_tables",
    )(positions.reshape(n, 1), invf, sgn64, sgn128)


_C_AQ = 0
_C_AK = _C_AQ + ATT_W
_C_AV = _C_AK + ATT_W
_C_IQ = _C_AV + ATT_W
_C_IK = _C_IQ + IDX_QW
_C_IW = _C_IK + LANES
_C_RQ = _C_IW + LANES
_C_RK = _C_RQ + RET_QKW
_C_RV = _C_RK + RET_QKW
_C_RG = _C_RV + RET_VW
_C_GA = _C_RG + RET_VW


def _pack_w_in(w_in, d_model):
    splits = (ATT_W, ATT_W, ATT_W, IDX_QW, IDX_HEAD_DIM, IDX_HEADS,
              RET_QKW, RET_QKW, RET_VW, RET_VW, d_model, d_model)
    cuts = np.cumsum(splits)[:-1].tolist()
    aq, ak, av, iq, ik, iw, rq, rk, rv, rg, ga, gb = jnp.split(w_in, cuts, axis=1)
    iw_pad = jnp.pad(iw, ((0, 0), (0, LANES - IDX_HEADS)))
    packed = jnp.concatenate([aq, ak, av, iq, ik, ik, iw_pad, rq, rk, rv, rg, ga, gb], axis=1)
    return packed.astype(BF16)


def _in_proj_kernel(x_ref, g_ref, w_ref, c64_ref, s64_ref, c128_ref, s128_ref,
                    aq_ref, ak_ref, avt_ref, iq_ref, ik_ref, iw_ref,
                    rq_ref, rk_ref, rv_ref, rgs_ref, sa_ref, sb_ref, *, d_model):
    tm = x_ref.shape[0]
    h = _rms(x_ref[...], g_ref[...]).astype(BF16)

    def proj(c0, width):
        return jnp.dot(h, w_ref[:, c0:c0 + width], preferred_element_type=F32)

    lane = lax.broadcasted_iota(jnp.int32, (tm, LANES), 1)
    first_half = (lane % ATT_HEAD_DIM) < (ATT_HEAD_DIM // 2)
    c64, s64 = c64_ref[...], s64_ref[...]
    c128, s128 = c128_ref[...], s128_ref[...]

    def rot64(y):
        out = []
        for j in range(y.shape[1] // LANES):
            yj = y[:, j * LANES:(j + 1) * LANES]
            partner = jnp.where(first_half, pltpu.roll(yj, LANES - 32, 1), pltpu.roll(yj, 32, 1))
            out.append(yj * c64 + partner * s64)
        return out[0] if len(out) == 1 else jnp.concatenate(out, axis=1)

    def rot128(y):
        out = []
        for j in range(y.shape[1] // LANES):
            yj = y[:, j * LANES:(j + 1) * LANES]
            out.append(yj * c128 + pltpu.roll(yj, 64, 1) * s128)
        return jnp.concatenate(out, axis=1)

    aq_ref[0] = (rot64(proj(_C_AQ, ATT_W)) * (ATT_HEAD_DIM ** -0.5 * LOG2_E)).T.astype(BF16)
    ak_ref[0] = rot64(proj(_C_AK, ATT_W)).astype(BF16)
    avt_ref[0] = proj(_C_AV, ATT_W).T.astype(BF16)
    iq_ref[0] = rot64(proj(_C_IQ, IDX_QW)).T.astype(BF16)
    ik_ref[0] = rot64(proj(_C_IK, LANES)).astype(BF16)
    iw_ref[0] = (proj(_C_IW, LANES) * (IDX_HEADS ** -0.5 * IDX_HEAD_DIM ** -0.5)).T
    rq_ref[0] = rot128(proj(_C_RQ, RET_QKW)).astype(BF16)
    rk_ref[0] = (rot128(proj(_C_RK, RET_QKW)) * (RET_QK_DIM ** -0.5)).astype(BF16)
    rv_ref[0] = proj(_C_RV, RET_VW).astype(BF16)
    rg = proj(_C_RG, RET_VW)
    rgs_ref[0] = rg * jax.nn.sigmoid(rg)
    sa_ref[0] = jax.nn.sigmoid(proj(_C_GA, d_model))
    sb_ref[0] = jax.nn.sigmoid(proj(_C_GA + d_model, d_model))


def _in_proj(x, gain, w_packed, tables):
    b, t, d = x.shape
    tm = min(ROW_TILE, t)
    nt = t // tm
    width = w_packed.shape[1]
    tok = lambda w: pl.BlockSpec((1, tm, w), lambda bi, ti: (bi, ti, 0))
    tab = pl.BlockSpec((tm, LANES), lambda bi, ti: (bi * nt + ti, 0))
    sds = lambda w, dt: jax.ShapeDtypeStruct((b, t, w), dt)
    tok_t = lambda w: pl.BlockSpec((1, w, tm), lambda bi, ti: (bi, 0, ti))
    sds_t = lambda w, dt: jax.ShapeDtypeStruct((b, w, t), dt)
    outs = [
        (tok_t(ATT_W), sds_t(ATT_W, BF16)),
        (tok(ATT_W), sds(ATT_W, BF16)),
        (tok_t(ATT_W), sds_t(ATT_W, BF16)),
        (tok_t(IDX_QW), sds_t(IDX_QW, BF16)),
        (tok(LANES), sds(LANES, BF16)),
        (tok_t(LANES), sds_t(LANES, F32)),
        (tok(RET_QKW), sds(RET_QKW, BF16)),
        (tok(RET_QKW), sds(RET_QKW, BF16)),
        (tok(RET_VW), sds(RET_VW, BF16)),
        (tok(RET_VW), sds(RET_VW, F32)),
        (tok(d), sds(d, F32)),
        (tok(d), sds(d, F32)),
    ]
    return pl.pallas_call(
        functools.partial(_in_proj_kernel, d_model=d),
        grid=(b, nt),
        in_specs=[pl.BlockSpec((None, tm, d), lambda bi, ti: (bi, ti, 0)),
                  _resident((1, d)), _resident((d, width)), tab, tab, tab, tab],
        out_specs=[o[0] for o in outs],
        out_shape=[o[1] for o in outs],
        compiler_params=_params("parallel", "parallel"),
        name="in_proj",
    )(x, gain, w_packed, *tables)


def _sparse_attn_kernel(iq_ref, iw_ref, aq_ref, ik_ref, ak_ref, avt_ref, out_ref,
                        s_ref, qi_ref, qa_ref, o_ref, m_ref, l_ref, pa_ref, pb_ref, mxa_ref, mxb_ref,
                        *, topk):
    tq, tk = ATT_TQ, ATT_TK
    qi = pl.program_id(1)
    nk = qi + 1

    def krows(kc):
        return pl.ds(pl.multiple_of(kc * tk, tk), tk)

    zeros = jnp.zeros((ATT_HEAD_DIM, tq), BF16)
    for h in range(ATT_HEADS):
        hrows = slice(h * ATT_HEAD_DIM, (h + 1) * ATT_HEAD_DIM)
        halves = [zeros, aq_ref[0, hrows, :]] if h % 2 else [aq_ref[0, hrows, :], zeros]
        qa_ref[h] = jnp.concatenate(halves, axis=0)
        qi_ref[h] = jnp.concatenate([iq_ref[0, hrows, :], zeros], axis=0)
    w_t = iw_ref[0]

    n_trips = (nk + 1) // 2
    last = nk - 1

    def score_dots(kc, dst_ref):
        k = ik_ref[0, krows(kc), :]
        for h in range(IDX_HEADS):
            dst_ref[h] = jnp.dot(k, qi_ref[h], preferred_element_type=F32)

    def score_reduce(src_ref, kc, bounds):
        acc = None
        for h in range(IDX_HEADS):
            term = w_t[h:h + 1, :] * jnp.maximum(src_ref[h], 0.0)
            acc = term if acc is None else acc + term
        s_ref[krows(kc), :] = acc
        groups = acc.reshape(tk // SUBLANES, SUBLANES, tq)
        return jnp.maximum(bounds[0], groups.max(axis=0)), jnp.minimum(bounds[1], groups.min(axis=0))

    def score_body(j, bounds):
        c0 = 2 * j
        c1 = jnp.minimum(c0 + 1, last)
        score_dots(c1, pb_ref)
        bounds = score_reduce(pa_ref, c0, bounds)
        score_dots(jnp.minimum(c0 + 2, last), pa_ref)
        return score_reduce(pb_ref, c1, bounds)

    score_dots(0, pa_ref)
    smax, smin = lax.fori_loop(
        0, n_trips, score_body,
        (jnp.full((SUBLANES, tq), NEG_INF, F32), jnp.full((SUBLANES, tq), -NEG_INF, F32)))
    smax = smax.max(axis=0, keepdims=True)
    smin = smin.min(axis=0, keepdims=True)

    key_in = lax.broadcasted_iota(jnp.int32, (tk, tq), 0)
    qry_in = lax.broadcasted_iota(jnp.int32, (tk, tq), 1)
    s_ref[krows(qi), :] = jnp.where(key_in <= qry_in, s_ref[krows(qi), :], NEG_INF)

    def count(hits):
        def body(kc, acc):
            hit = hits(s_ref[krows(kc), :], kc)
            return acc + hit.reshape(tk // CNT_ROWS, CNT_ROWS, tq).sum(axis=0)
        acc = lax.fori_loop(0, nk, body, jnp.zeros((CNT_ROWS, tq), jnp.int32))
        return acc.sum(axis=0, keepdims=True)

    def count_ge(cand_f):
        def body(kc, acc):
            s = s_ref[krows(kc), :]
            for g in range(tk // CNT_ROWS):
                acc = jnp.where(s[g * CNT_ROWS:(g + 1) * CNT_ROWS] >= cand_f, acc + 1, acc)
            return acc
        acc = lax.fori_loop(0, nk, body, jnp.zeros((CNT_ROWS, tq), jnp.int32))
        return acc.sum(axis=0, keepdims=True)

    def flag(cond):
        return jnp.where(cond, 1.0, 0.0)

    n_keys = qi * tq + lax.broadcasted_iota(jnp.int32, (1, tq), 1) + 1
    at_zero = count_ge(jnp.zeros((1, tq), F32))
    above_zero = count_ge(jnp.full((1, tq), F32_TINY, F32))
    positive = above_zero >= topk
    negative = at_zero < topk
    lo = jnp.where(positive, F32_TINY, jnp.where(negative, smin, 0.0))
    cnt_lo = jnp.where(positive, above_zero, jnp.where(negative, n_keys, at_zero))
    hi = jnp.where(positive, 2.0 * smax, jnp.where(negative, -0.0, 0.0))
    few = n_keys < topk
    settled = jnp.maximum(flag(few), (1.0 - flag(positive)) * (1.0 - flag(negative)))

    def search_cond(carry):
        p, _, _, _, done = carry
        return (p < SEARCH_MAX_PASSES) & (done < 0.5)

    def bisect(lo, hi, cnt_lo):
        closed = jnp.zeros((1, tq), F32)
        for _ in range(SEARCH_UNROLL):
            mid = 0.5 * lo + 0.5 * hi
            tot = count_ge(mid)
            ok = tot >= topk
            closed = jnp.maximum(closed, jnp.maximum(flag(mid <= lo), flag(mid >= hi)))
            lo = jnp.where(ok, mid, lo)
            cnt_lo = jnp.where(ok, tot, cnt_lo)
            hi = jnp.where(ok, hi, mid)
        return lo, hi, cnt_lo, closed

    def warm_body(_, carry):
        return bisect(*carry)[:3]

    def search_body(carry):
        p, lo, hi, cnt_lo, _ = carry
        lo, hi, cnt_lo, closed = bisect(lo, hi, cnt_lo)
        done = jnp.min(jnp.maximum(jnp.maximum(settled, closed), flag(cnt_lo == topk)))
        return p + SEARCH_UNROLL, lo, hi, cnt_lo, done

    lo, hi, cnt_lo = lax.fori_loop(0, SEARCH_WARM_TRIPS, warm_body, (lo, hi, cnt_lo))
    done0 = jnp.min(jnp.maximum(settled, flag(cnt_lo == topk)))
    _, lo, _, cnt_lo, _ = lax.while_loop(
        search_cond, search_body, (jnp.int32(SEARCH_WARM_TRIPS * SEARCH_UNROLL), lo, hi, cnt_lo, done0))
    tau = jnp.where(few, F32_LOWEST, lo)

    tied = (1.0 - flag(few)) * flag(cnt_lo > topk)

    @pl.when(jnp.max(tied) > 0.5)
    def _():
        need = (topk - count(lambda s, kc: jnp.where(s > tau, 1, 0))).astype(F32)
        tri = jnp.where(lax.broadcasted_iota(jnp.int32, (tk, tk), 1)
                        <= lax.broadcasted_iota(jnp.int32, (tk, tk), 0), 1.0, 0.0).astype(BF16)

        def drop_body(kc, seen):
            s = s_ref[krows(kc), :]
            eq = jnp.where(s == tau, 1.0, 0.0)
            rank = seen + jnp.dot(tri, eq.astype(BF16), preferred_element_type=F32)
            s_ref[krows(kc), :] = jnp.where(eq * rank > need, NEG_INF, s)
            return rank[tk - 1:tk, :]

        lax.fori_loop(0, nk, drop_body, jnp.zeros((1, tq), F32))

    def bias_body(kc, carry):
        s_ref[krows(kc), :] = jnp.where(s_ref[krows(kc), :] >= tau, 0.0, NEG_INF)
        return carry

    lax.fori_loop(0, nk, bias_body, 0)

    m_ref[...] = jnp.full(m_ref.shape, NEG_INF, F32)
    l_ref[...] = jnp.zeros(l_ref.shape, F32)
    o_ref[...] = jnp.zeros(o_ref.shape, F32)

    def logit_dots(kc, dst_ref, dmax_ref, live):
        bias = s_ref[krows(kc), :]
        if live is not True:
            bias = jnp.where(live, bias, NEG_INF)
        for h in range(ATT_HEADS):
            pair = h // 2
            k = ak_ref[0, krows(kc), pair * LANES:(pair + 1) * LANES]
            lg = jnp.dot(k, qa_ref[h], preferred_element_type=F32) + bias
            dst_ref[h] = lg
            dmax_ref[h:h + 1, :] = lg.max(axis=0, keepdims=True)

    def softmax_pv(src_ref, smax_ref, kc):
        for h in range(ATT_HEADS):
            hrows = slice(h * ATT_HEAD_DIM, (h + 1) * ATT_HEAD_DIM)
            m_old = m_ref[h:h + 1, :]
            m_new = jnp.maximum(m_old, smax_ref[h:h + 1, :])
            m_safe = jnp.where(m_new == NEG_INF, 0.0, m_new)
            p = jnp.exp2(src_ref[h] - m_safe)
            alpha = jnp.exp2(m_old - m_safe)
            m_ref[h:h + 1, :] = m_new
            l_ref[h:h + 1, :] = alpha * l_ref[h:h + 1, :] + p.sum(axis=0, keepdims=True)
            v_t = avt_ref[0, hrows, krows(kc)]
            o_ref[hrows, :] = alpha * o_ref[hrows, :] + jnp.dot(v_t, p.astype(BF16), preferred_element_type=F32)

    def attn_body(j, carry):
        c0 = 2 * j
        c1 = jnp.minimum(c0 + 1, last)
        logit_dots(c1, pb_ref, mxb_ref, c0 + 1 <= last)
        softmax_pv(pa_ref, mxa_ref, c0)
        logit_dots(jnp.minimum(c0 + 2, last), pa_ref, mxa_ref, True)
        softmax_pv(pb_ref, mxb_ref, c1)
        return carry

    logit_dots(0, pa_ref, mxa_ref, True)
    lax.fori_loop(0, n_trips, attn_body, 0)
    for h in range(ATT_HEADS):
        hrows = slice(h * ATT_HEAD_DIM, (h + 1) * ATT_HEAD_DIM)
        o_ref[hrows, :] = o_ref[hrows, :] / l_ref[h:h + 1, :]
    out_ref[0] = o_ref[...].T.astype(BF16)


def _sparse_attention(aq_t, ak, av_t, iq_t, ik2, iw_t):
    b, t, _ = ak.shape
    tq = ATT_TQ
    assert t % tq == 0 and ATT_TQ == ATT_TK
    topk = min(TOPK_MAX, t // 4)
    qblk_t = lambda w: pl.BlockSpec((1, w, tq), lambda bi, qi: (bi, 0, qi))
    full = lambda r, c: pl.BlockSpec((1, r, c), lambda bi, qi: (bi, 0, 0))
    return pl.pallas_call(
        functools.partial(_sparse_attn_kernel, topk=topk),
        grid=(b, t // tq),
        in_specs=[qblk_t(IDX_QW), qblk_t(LANES), qblk_t(ATT_W),
                  full(t, LANES), full(t, ATT_W), full(ATT_W, t)],
        out_specs=pl.BlockSpec((1, tq, ATT_W), lambda bi, qi: (bi, qi, 0)),
        out_shape=jax.ShapeDtypeStruct((b, t, ATT_W), BF16),
        scratch_shapes=[
            pltpu.VMEM((t, tq), F32),
            pltpu.VMEM((IDX_HEADS, LANES, tq), BF16),
            pltpu.VMEM((ATT_HEADS, LANES, tq), BF16),
            pltpu.VMEM((ATT_W, tq), F32),
            pltpu.VMEM((ATT_HEADS, tq), F32),
            pltpu.VMEM((ATT_HEADS, tq), F32),
            pltpu.VMEM((ATT_HEADS, ATT_TK, tq), F32),
            pltpu.VMEM((ATT_HEADS, ATT_TK, tq), F32),
            pltpu.VMEM((ATT_HEADS, tq), F32),
            pltpu.VMEM((ATT_HEADS, tq), F32),
        ],
        compiler_params=_params("parallel", "parallel"),
        name="sparse_attn",
    )(iq_t, iw_t, aq_t, ik2, ak, av_t)


def _retention_kernel(rq_ref, rk_ref, rv_ref, rgs_ref, gain_ref, decay_ref, xi_ref, zeta_ref, cd_ref,
                      out_ref, state_ref):
    c = RET_CHUNK

    @pl.when(pl.program_id(1) == 0)
    def _():
        state_ref[...] = jnp.zeros_like(state_ref)

    def chunk_body(ci, carry):
        rows = pl.ds(pl.multiple_of(ci * c, c), c)
        heads = range(RET_HEADS)
        qk_cols = [slice(h * RET_QK_DIM, (h + 1) * RET_QK_DIM) for h in heads]
        v_slices = [slice(h * RET_V_DIM, (h + 1) * RET_V_DIM) for h in heads]
        q = [rq_ref[0, rows, qk_cols[h]] for h in heads]
        k = [rk_ref[0, rows, qk_cols[h]] for h in heads]
        v = [rv_ref[0, rows, v_slices[h]] for h in heads]
        att = [lax.dot_general(q[h], k[h], (((1,), (1,)), ((), ())), preferred_element_type=F32)
               for h in heads]
        cross = [jnp.dot(q[h], state_ref[h].astype(BF16), preferred_element_type=F32) * xi_ref[h]
                 for h in heads]
        for h in heads:
            kz_t = (k[h].astype(F32) * zeta_ref[h]).T.astype(BF16)
            state_ref[h] = (state_ref[h] * cd_ref[h, 0:1, :]
                            + jnp.dot(kz_t, v[h], preferred_element_type=F32))
        for h in heads:
            v_cols = v_slices[h]
            inner = jnp.dot((att[h] * decay_ref[h]).astype(BF16), v[h], preferred_element_type=F32)
            o = inner + cross[h]
            mu = jnp.mean(o, axis=-1, keepdims=True)
            dev = o - mu
            var = jnp.mean(dev * dev, axis=-1, keepdims=True)
            y = dev * lax.rsqrt(var + EPS) * gain_ref[:, v_cols]
            out_ref[0, rows, v_cols] = (rgs_ref[0, rows, v_cols] * y).astype(BF16)
        return carry

    lax.fori_loop(0, rq_ref.shape[1] // c, chunk_body, 0)


def _retention(rq, rk, rv, rgs, gn_gain):
    b, t, _ = rq.shape
    rb = min(RET_ROWS, t)
    c = RET_CHUNK
    log_gamma = jnp.log(1.0 - 2.0 ** (-5.0 - jnp.arange(RET_HEADS, dtype=F32)))
    i = jnp.arange(c, dtype=F32)
    diff = i[:, None] - i[None, :]
    decay = jnp.where(diff[None] >= 0,
                      jnp.exp(jnp.maximum(diff, 0.0)[None] * log_gamma[:, None, None]), 0.0)
    xi = jnp.exp((i + 1.0)[None, :] * log_gamma[:, None])
    zeta = jnp.exp((c - 1.0 - i)[None, :] * log_gamma[:, None])
    xi_b = jnp.broadcast_to(xi[:, :, None], (RET_HEADS, c, RET_V_DIM))
    zeta_b = jnp.broadcast_to(zeta[:, :, None], (RET_HEADS, c, RET_QK_DIM))
    cd_b = jnp.broadcast_to(jnp.exp(c * log_gamma)[:, None, None], (RET_HEADS, SUBLANES, RET_V_DIM))
    tok = lambda w: pl.BlockSpec((1, rb, w), lambda bi, ti: (bi, ti, 0))
    return pl.pallas_call(
        _retention_kernel,
        grid=(b, t // rb),
        in_specs=[tok(RET_QKW), tok(RET_QKW), tok(RET_VW), tok(RET_VW), _resident((1, RET_VW)),
                  _resident(decay.shape), _resident(xi_b.shape), _resident(zeta_b.shape),
                  _resident(cd_b.shape)],
        out_specs=tok(RET_VW),
        out_shape=jax.ShapeDtypeStruct((b, t, RET_VW), BF16),
        scratch_shapes=[pltpu.VMEM((RET_HEADS, RET_QK_DIM, RET_V_DIM), F32)],
        compiler_params=_params("parallel", "arbitrary"),
        name="retention",
    )(rq, rk, rv, rgs, gn_gain, decay, xi_b, zeta_b, cd_b)


def _mix_out_kernel(attn_ref, gret_ref, sa_ref, sb_ref, x_ref, wa_ref, wr_ref, wo_ref,
                    g_post_ref, g_pre_ref, x1_ref, h2_ref):
    y_a = jnp.dot(attn_ref[...], wa_ref[...], preferred_element_type=F32)
    y_b = jnp.dot(gret_ref[...], wr_ref[...], preferred_element_type=F32)
    merged = sa_ref[...] * y_a + sb_ref[...] * y_b
    m = jnp.dot(merged.astype(BF16), wo_ref[...], preferred_element_type=F32)
    x1 = x_ref[...] + _rms(m, g_post_ref[...])
    x1_ref[...] = x1
    h2_ref[...] = _rms(x1, g_pre_ref[...]).astype(BF16)


def _mix_out(attn, gret, sa, sb, x, w_br_attn, w_br_ret, w_out, g_post, g_pre_ffn):
    b, t, d = x.shape
    tm = min(ROW_TILE, t)
    tok = lambda w: pl.BlockSpec((None, tm, w), lambda bi, ti: (bi, ti, 0))
    return pl.pallas_call(
        _mix_out_kernel,
        grid=(b, t // tm),
        in_specs=[tok(ATT_W), tok(RET_VW), tok(d), tok(d), tok(d),
                  _resident(w_br_attn.shape), _resident(w_br_ret.shape), _resident(w_out.shape),
                  _resident((1, d)), _resident((1, d))],
        out_specs=[tok(d), tok(d)],
        out_shape=[jax.ShapeDtypeStruct((b, t, d), F32), jax.ShapeDtypeStruct((b, t, d), BF16)],
        compiler_params=_params("parallel", "parallel"),
        name="mix_out",
    )(attn, gret, sa, sb, x, w_br_attn, w_br_ret, w_out, g_post, g_pre_ffn)


def _conv_ffn_kernel(h_ref, x1_ref, wup_ref, cw_ref, cb_ref, wdn_ref, g_ref, out_ref, carry_ref, *, d_ff):
    tm = h_ref.shape[0]

    @pl.when(pl.program_id(1) == 0)
    def _():
        carry_ref[...] = jnp.zeros_like(carry_ref)

    def up_proj(c0, width):
        return (jnp.dot(h_ref[...], wup_ref[:, c0:c0 + width], preferred_element_type=F32),
                jnp.dot(h_ref[...], wup_ref[:, d_ff + c0:d_ff + c0 + width], preferred_element_type=F32))

    def shifted(u, prev, k):
        row = lax.broadcasted_iota(jnp.int32, (SUBLANES, u.shape[1]), 0)
        r = pltpu.roll(u, k, 0)
        head = r[:SUBLANES]
        for j in range(k):
            head = jnp.where(row == j, prev[SUBLANES - k + j:SUBLANES - k + j + 1, :], head)
        return jnp.concatenate([head, r[SUBLANES:]], axis=0)

    def conv(u, c0):
        cols = slice(c0, c0 + u.shape[1])
        prev = carry_ref[:, cols]
        carry_ref[:, cols] = u[tm - SUBLANES:, :]
        return (cw_ref[0:1, cols] * shifted(u, prev, 2) + cw_ref[1:2, cols] * shifted(u, prev, 1)
                + cw_ref[2:3, cols] * u + cb_ref[:, cols])

    chunks = [(c0, min(FFN_CHUNK, d_ff - c0)) for c0 in range(0, d_ff, FFN_CHUNK)]
    acc = jnp.zeros((tm, out_ref.shape[1]), F32)
    pre = up_proj(*chunks[0])
    for i, (c0, width) in enumerate(chunks):
        nxt = up_proj(*chunks[i + 1]) if i + 1 < len(chunks) else None
        gate = conv(pre[0], c0)
        up = conv(pre[1], d_ff + c0)
        act = (gate * jax.nn.sigmoid(gate) * up).astype(BF16)
        acc = acc + jnp.dot(act, wdn_ref[c0:c0 + width, :], preferred_element_type=F32)
        pre = nxt
    out_ref[...] = x1_ref[...] + _rms(acc, g_ref[...])


def _conv_ffn(h2, x1, w_up, conv_w, conv_b, w_down, g_post):
    b, t, d = x1.shape
    d_ff = w_down.shape[0]
    assert d_ff % LANES == 0 and conv_w.shape[0] == CONV_WIDTH
    tm = min(ROW_TILE, t)
    tok = pl.BlockSpec((None, tm, d), lambda bi, ti: (bi, ti, 0))
    return pl.pallas_call(
        functools.partial(_conv_ffn_kernel, d_ff=d_ff),
        grid=(b, t // tm),
        in_specs=[tok, tok, _resident(w_up.shape), _resident(conv_w.shape), _resident((1, 2 * d_ff)),
                  _resident(w_down.shape), _resident((1, d))],
        out_specs=tok,
        out_shape=jax.ShapeDtypeStruct((b, t, d), F32),
        scratch_shapes=[pltpu.VMEM((SUBLANES, 2 * d_ff), F32)],
        compiler_params=_params("parallel", "arbitrary"),
        name="conv_ffn",
    )(h2, x1, w_up, conv_w, conv_b, w_down, g_post)


def _layer(x, positions, norm_pre_mix, w_in, w_br_attn, w_br_ret, ret_gn_gain, w_out,
           norm_post_mix, norm_pre_ffn, w_ffn_up, conv_w, conv_b, w_ffn_down, norm_post_ffn):
    d = x.shape[-1]
    row = lambda v: v.reshape(1, -1)
    tables = _rope_tables(positions)
    (aq, ak, avt, iq, ik2, iw, rq, rk, rv, rgs, sa, sb) = _in_proj(
        x, row(norm_pre_mix), _pack_w_in(w_in, d), tables)
    attn = _sparse_attention(aq, ak, avt, iq, ik2, iw)
    gret = _retention(rq, rk, rv, rgs, row(ret_gn_gain))
    x1, h2 = _mix_out(attn, gret, sa, sb, x, w_br_attn.astype(BF16), w_br_ret.astype(BF16),
                      w_out.astype(BF16), row(norm_post_mix), row(norm_pre_ffn))
    return _conv_ffn(h2, x1, w_ffn_up.astype(BF16), conv_w, row(conv_b), w_ffn_down.astype(BF16),
                     row(norm_post_ffn))


def kernel(x, positions, norm_pre_mix, w_in, w_br_attn, w_br_ret, ret_gn_gain, w_out, norm_post_mix,
           norm_pre_ffn, w_ffn_up, conv_w, conv_b, w_ffn_down, norm_post_ffn):
    for l in range(w_in.shape[0]):
        x = _layer(x, positions, norm_pre_mix[l], w_in[l], w_br_attn[l], w_br_ret[l], ret_gn_gain[l],
                   w_out[l], norm_post_mix[l], norm_pre_ffn[l], w_ffn_up[l], conv_w[l], conv_b[l],
                   w_ffn_down[l], norm_post_ffn[l])
    return x
```

```python
import functools

import jax
import jax.numpy as jnp
import numpy as np
from jax import lax
from jax.experimental import pallas as pl
from jax.experimental.pallas import tpu as pltpu

EPS = 1e-6
ROPE_THETA = 10000.0
ATT_HEADS = 8
ATT_HEAD_DIM = 64
IDX_HEADS = 8
IDX_HEAD_DIM = 64
TOPK_MAX = 256
RET_HEADS = 4
RET_QK_DIM = 128
RET_V_DIM = 256
RET_CHUNK = 128
CONV_WIDTH = 3

ATT_W = ATT_HEADS * ATT_HEAD_DIM
ATT_V_ROWS = ATT_HEAD_DIM + 16
IDX_QW = IDX_HEADS * IDX_HEAD_DIM
RET_QKW = RET_HEADS * RET_QK_DIM
RET_VW = RET_HEADS * RET_V_DIM

LANES = 128
SUBLANES = 8
VMEM_LIMIT = 56 * 1024 * 1024

ROW_TILE = 512
TABLE_TILE = 1024
ATT_TQ = 256
ATT_TK = 256
CNT_ROWS = 32
SEARCH_UNROLL = 4
SEARCH_WARM_TRIPS = 3
SEARCH_MAX_PASSES = 288
RET_ROWS = 512
FFN_CHUNK = 512

F32 = jnp.float32
BF16 = jnp.bfloat16
NEG_INF = float("-inf")
F32_LOWEST = float(np.finfo(np.float32).min)
F32_TINY = float(np.finfo(np.float32).tiny)
LOG2_E = float(np.log2(np.e))


def _params(*semantics):
    return pltpu.CompilerParams(dimension_semantics=semantics, vmem_limit_bytes=VMEM_LIMIT)


def _resident(shape):
    zeros = (0,) * len(shape)
    return pl.BlockSpec(shape, lambda *_: zeros, pipeline_mode=pl.Buffered(1))


def _rms(x, gain):
    return x * lax.rsqrt(jnp.mean(x * x, axis=-1, keepdims=True) + EPS) * gain


def _rope_table_kernel(pos_ref, invf_ref, sgn64_ref, sgn128_ref, c64_ref, s64_ref, c128_ref, s128_ref):
    ang = pos_ref[...].astype(F32) * invf_ref[...]
    c = jnp.cos(ang)
    s = jnp.sin(ang)
    c_lo, c_hi = c[:, :64], c[:, 64:]
    s_lo, s_hi = s[:, :64], s[:, 64:]
    c64_ref[...] = jnp.concatenate([c_lo, c_lo], axis=1)
    s64_ref[...] = jnp.concatenate([s_lo, s_lo], axis=1) * sgn64_ref[...]
    c128_ref[...] = jnp.concatenate([c_hi, c_hi], axis=1)
    s128_ref[...] = jnp.concatenate([s_hi, s_hi], axis=1) * sgn128_ref[...]


def _rope_tables(positions):
    n = positions.size
    tm = min(TABLE_TILE, n)
    half64, half128 = ATT_HEAD_DIM // 2, RET_QK_DIM // 2
    f64 = ROPE_THETA ** (-(jnp.arange(half64, dtype=F32) * 2.0 / ATT_HEAD_DIM))
    f128 = ROPE_THETA ** (-(jnp.arange(half128, dtype=F32) * 2.0 / RET_QK_DIM))
    invf = jnp.concatenate([f64, f64, f128])[None, :]
    sgn64 = jnp.tile(jnp.concatenate([-jnp.ones(half64, F32), jnp.ones(half64, F32)]), 2)[None, :]
    sgn128 = jnp.concatenate([-jnp.ones(half128, F32), jnp.ones(half128, F32)])[None, :]
    row = pl.BlockSpec((tm, LANES), lambda i: (i, 0))
    const = pl.BlockSpec((1, LANES), lambda i: (0, 0))
    table = jax.ShapeDtypeStruct((n, LANES), F32)
    return pl.pallas_call(
        _rope_table_kernel,
        grid=(n // tm,),
        in_specs=[pl.BlockSpec((tm, 1), lambda i: (i, 0)), const, const, const],
        out_specs=[row, row, row, row],
        out_shape=[table, table, table, table],
        compiler_params=_params("parallel"),
        name="rope_tables",
    )(positions.reshape(n, 1), invf, sgn64, sgn128)


_C_AQ = 0
_C_AK = _C_AQ + ATT_W
_C_AV = _C_AK + ATT_W
_C_IQ = _C_AV + ATT_W
_C_IK = _C_IQ + IDX_QW
_C_IW = _C_IK + LANES
_C_RQ = _C_IW + LANES
_C_RK = _C_RQ + RET_QKW
_C_RV = _C_RK + RET_QKW
_C_RG = _C_RV + RET_VW
_C_GA = _C_RG + RET_VW


def _pack_w_in(w_in, d_model):
    splits = (ATT_W, ATT_W, ATT_W, IDX_QW, IDX_HEAD_DIM, IDX_HEADS,
              RET_QKW, RET_QKW, RET_VW, RET_VW, d_model, d_model)
    cuts = np.cumsum(splits)[:-1].tolist()
    aq, ak, av, iq, ik, iw, rq, rk, rv, rg, ga, gb = jnp.split(w_in.astype(BF16), cuts, axis=1)
    iw_pad = jnp.pad(iw, ((0, 0), (0, LANES - IDX_HEADS)))
    return jnp.concatenate([aq, ak, av, iq, ik, ik, iw_pad, rq, rk, rv, rg, ga, gb], axis=1)


def _in_proj_kernel(x_ref, g_ref, w_ref, c64_ref, s64_ref, c128_ref, s128_ref,
                    aq_ref, ak_ref, avt_ref, iq_ref, ik_ref, iw_ref,
                    rq_ref, rk_ref, rv_ref, rgs_ref, sa_ref, sb_ref, *, d_model):
    tm = x_ref.shape[0]
    h = _rms(x_ref[...], g_ref[...]).astype(BF16)

    def proj(c0, width):
        return jnp.dot(h, w_ref[:, c0:c0 + width], preferred_element_type=F32)

    lane = lax.broadcasted_iota(jnp.int32, (tm, LANES), 1)
    first_half = (lane % ATT_HEAD_DIM) < (ATT_HEAD_DIM // 2)
    c64, s64 = c64_ref[...], s64_ref[...]
    c128, s128 = c128_ref[...], s128_ref[...]

    def rot64(y):
        out = []
        for j in range(y.shape[1] // LANES):
            yj = y[:, j * LANES:(j + 1) * LANES]
            partner = jnp.where(first_half, pltpu.roll(yj, LANES - 32, 1), pltpu.roll(yj, 32, 1))
            out.append(yj * c64 + partner * s64)
        return out[0] if len(out) == 1 else jnp.concatenate(out, axis=1)

    def rot128(y):
        out = []
        for j in range(y.shape[1] // LANES):
            yj = y[:, j * LANES:(j + 1) * LANES]
            out.append(yj * c128 + pltpu.roll(yj, 64, 1) * s128)
        return jnp.concatenate(out, axis=1)

    aq_ref[0] = (rot64(proj(_C_AQ, ATT_W)) * (ATT_HEAD_DIM ** -0.5 * LOG2_E)).T.astype(BF16)
    ak_ref[0] = rot64(proj(_C_AK, ATT_W)).astype(BF16)
    v_t = proj(_C_AV, ATT_W).T
    ones = jnp.ones((ATT_V_ROWS - ATT_HEAD_DIM, tm), F32)
    avt_ref[0] = jnp.concatenate(
        [part for h in range(ATT_HEADS) for part in (v_t[h * ATT_HEAD_DIM:(h + 1) * ATT_HEAD_DIM], ones)],
        axis=0).astype(BF16)
    iq_ref[0] = rot64(proj(_C_IQ, IDX_QW)).T.astype(BF16)
    ik_ref[0] = rot64(proj(_C_IK, LANES)).astype(BF16)
    iw_ref[0] = (proj(_C_IW, LANES) * (IDX_HEADS ** -0.5 * IDX_HEAD_DIM ** -0.5)).T
    rq_ref[0] = rot128(proj(_C_RQ, RET_QKW)).astype(BF16)
    rk_ref[0] = (rot128(proj(_C_RK, RET_QKW)) * (RET_QK_DIM ** -0.5)).astype(BF16)
    rv_ref[0] = proj(_C_RV, RET_VW).astype(BF16)
    rg = proj(_C_RG, RET_VW)
    rgs_ref[0] = rg * jax.nn.sigmoid(rg)
    sa_ref[0] = jax.nn.sigmoid(proj(_C_GA, d_model))
    sb_ref[0] = jax.nn.sigmoid(proj(_C_GA + d_model, d_model))


def _in_proj(x, gain, w_packed, tables):
    b, t, d = x.shape
    tm = min(ROW_TILE, t)
    nt = t // tm
    width = w_packed.shape[1]
    tok = lambda w: pl.BlockSpec((1, tm, w), lambda bi, ti: (bi, ti, 0))
    tab = pl.BlockSpec((tm, LANES), lambda bi, ti: (bi * nt + ti, 0))
    sds = lambda w, dt: jax.ShapeDtypeStruct((b, t, w), dt)
    tok_t = lambda w: pl.BlockSpec((1, w, tm), lambda bi, ti: (bi, 0, ti))
    sds_t = lambda w, dt: jax.ShapeDtypeStruct((b, w, t), dt)
    outs = [
        (tok_t(ATT_W), sds_t(ATT_W, BF16)),
        (tok(ATT_W), sds(ATT_W, BF16)),
        (tok_t(ATT_HEADS * ATT_V_ROWS), sds_t(ATT_HEADS * ATT_V_ROWS, BF16)),
        (tok_t(IDX_QW), sds_t(IDX_QW, BF16)),
        (tok(LANES), sds(LANES, BF16)),
        (tok_t(LANES), sds_t(LANES, F32)),
        (tok(RET_QKW), sds(RET_QKW, BF16)),
        (tok(RET_QKW), sds(RET_QKW, BF16)),
        (tok(RET_VW), sds(RET_VW, BF16)),
        (tok(RET_VW), sds(RET_VW, F32)),
        (tok(d), sds(d, F32)),
        (tok(d), sds(d, F32)),
    ]
    return pl.pallas_call(
        functools.partial(_in_proj_kernel, d_model=d),
        grid=(b, nt),
        in_specs=[pl.BlockSpec((None, tm, d), lambda bi, ti: (bi, ti, 0)),
                  _resident((1, d)), _resident((d, width)), tab, tab, tab, tab],
        out_specs=[o[0] for o in outs],
        out_shape=[o[1] for o in outs],
        compiler_params=_params("parallel", "parallel"),
        name="in_proj",
    )(x, gain, w_packed, *tables)


def _sparse_attn_kernel(iq_ref, iw_ref, aq_ref, ik_ref, ak_ref, avt_ref, out_ref,
                        s_ref, qi_ref, qa_ref, o_ref, m_ref, pa_ref, pb_ref, mxa_ref, mxb_ref, bnd_ref,
                        *, topk):
    tq, tk = ATT_TQ, ATT_TK
    qi = pl.program_id(1)
    nk = qi + 1

    def krows(kc):
        return pl.ds(pl.multiple_of(kc * tk, tk), tk)

    zeros = jnp.zeros((ATT_HEAD_DIM, tq), BF16)
    for h in range(ATT_HEADS):
        hrows = slice(h * ATT_HEAD_DIM, (h + 1) * ATT_HEAD_DIM)
        halves = [zeros, aq_ref[0, hrows, :]] if h % 2 else [aq_ref[0, hrows, :], zeros]
        qa_ref[h] = jnp.concatenate(halves, axis=0)
        qi_ref[h] = jnp.concatenate([iq_ref[0, hrows, :], zeros], axis=0)
    w_t = iw_ref[0]

    full_trips = (nk - 1) // 2
    tail0 = 2 * full_trips
    tail_pair = (nk - tail0) == 2

    def score_dots(kc, dst_ref):
        k = ik_ref[0, krows(kc), :]
        for h in range(IDX_HEADS):
            dst_ref[h] = jnp.dot(k, qi_ref[h], preferred_element_type=F32)

    def score_reduce(src_ref, kc, diagonal=False):
        acc = None
        for h in range(IDX_HEADS):
            term = w_t[h:h + 1, :] * jnp.maximum(src_ref[h], 0.0)
            acc = term if acc is None else acc + term
        groups = acc.reshape(tk // SUBLANES, SUBLANES, tq)
        bnd_ref[0] = jnp.maximum(bnd_ref[0], groups.max(axis=0))
        bnd_ref[1] = jnp.minimum(bnd_ref[1], groups.min(axis=0))
        if diagonal:
            acc = jnp.where(lax.broadcasted_iota(jnp.int32, (tk, tq), 0)
                            <= lax.broadcasted_iota(jnp.int32, (tk, tq), 1), acc, NEG_INF)
        s_ref[krows(kc), :] = acc

    def score_body(j, carry):
        c0 = 2 * j
        score_dots(c0 + 1, pb_ref)
        score_reduce(pa_ref, c0)
        score_dots(c0 + 2, pa_ref)
        score_reduce(pb_ref, c0 + 1)
        return carry

    bnd_ref[0] = jnp.full((SUBLANES, tq), NEG_INF, F32)
    bnd_ref[1] = jnp.full((SUBLANES, tq), -NEG_INF, F32)
    score_dots(0, pa_ref)
    lax.fori_loop(0, full_trips, score_body, 0)

    @pl.when(tail_pair)
    def _():
        score_dots(tail0 + 1, pb_ref)
        score_reduce(pa_ref, tail0)
        score_reduce(pb_ref, tail0 + 1, diagonal=True)

    @pl.when(jnp.logical_not(tail_pair))
    def _():
        score_reduce(pa_ref, tail0, diagonal=True)

    smax = bnd_ref[0].max(axis=0, keepdims=True)
    smin = bnd_ref[1].min(axis=0, keepdims=True)

    def count(hits):
        def body(kc, acc):
            hit = hits(s_ref[krows(kc), :], kc)
            return acc + hit.reshape(tk // CNT_ROWS, CNT_ROWS, tq).sum(axis=0)
        acc = lax.fori_loop(0, nk, body, jnp.zeros((CNT_ROWS, tq), jnp.int32))
        return acc.sum(axis=0, keepdims=True)

    def count_ge(cand_f):
        def body(kc, acc):
            s = s_ref[krows(kc), :]
            for g in range(tk // CNT_ROWS):
                acc = jnp.where(s[g * CNT_ROWS:(g + 1) * CNT_ROWS] >= cand_f, acc + 1, acc)
            return acc
        acc = lax.fori_loop(0, nk, body, jnp.zeros((CNT_ROWS, tq), jnp.int32))
        return acc.sum(axis=0, keepdims=True)

    def flag(cond):
        return jnp.where(cond, 1.0, 0.0)

    n_keys = qi * tq + lax.broadcasted_iota(jnp.int32, (1, tq), 1) + 1
    at_zero = count_ge(jnp.zeros((1, tq), F32))
    above_zero = count_ge(jnp.full((1, tq), F32_TINY, F32))
    positive = above_zero >= topk
    negative = at_zero < topk
    lo = jnp.where(positive, F32_TINY, jnp.where(negative, smin, 0.0))
    cnt_lo = jnp.where(positive, above_zero, jnp.where(negative, n_keys, at_zero))
    hi = jnp.where(positive, 2.0 * smax, jnp.where(negative, -0.0, 0.0))
    few = n_keys < topk
    settled = jnp.maximum(flag(few), (1.0 - flag(positive)) * (1.0 - flag(negative)))

    def search_cond(carry):
        p, _, _, _, done = carry
        return (p < SEARCH_MAX_PASSES) & (done < 0.5)

    def bisect(lo, hi, cnt_lo):
        closed = jnp.zeros((1, tq), F32)
        for _ in range(SEARCH_UNROLL):
            mid = 0.5 * lo + 0.5 * hi
            tot = count_ge(mid)
            ok = tot >= topk
            closed = jnp.maximum(closed, jnp.maximum(flag(mid <= lo), flag(mid >= hi)))
            lo = jnp.where(ok, mid, lo)
            cnt_lo = jnp.where(ok, tot, cnt_lo)
            hi = jnp.where(ok, hi, mid)
        return lo, hi, cnt_lo, closed

    def warm_body(_, carry):
        return bisect(*carry)[:3]

    def search_body(carry):
        p, lo, hi, cnt_lo, _ = carry
        lo, hi, cnt_lo, closed = bisect(lo, hi, cnt_lo)
        done = jnp.min(jnp.maximum(jnp.maximum(settled, closed), flag(cnt_lo == topk)))
        return p + SEARCH_UNROLL, lo, hi, cnt_lo, done

    lo, hi, cnt_lo = lax.fori_loop(0, SEARCH_WARM_TRIPS, warm_body, (lo, hi, cnt_lo))
    done0 = jnp.min(jnp.maximum(settled, flag(cnt_lo == topk)))
    _, lo, _, cnt_lo, _ = lax.while_loop(
        search_cond, search_body, (jnp.int32(SEARCH_WARM_TRIPS * SEARCH_UNROLL), lo, hi, cnt_lo, done0))
    tau = jnp.where(few, F32_LOWEST, lo)

    tied = (1.0 - flag(few)) * flag(cnt_lo > topk)

    @pl.when(jnp.max(tied) > 0.5)
    def _():
        need = (topk - count(lambda s, kc: jnp.where(s > tau, 1, 0))).astype(F32)
        tri = jnp.where(lax.broadcasted_iota(jnp.int32, (tk, tk), 1)
                        <= lax.broadcasted_iota(jnp.int32, (tk, tk), 0), 1.0, 0.0).astype(BF16)

        def drop_body(kc, seen):
            s = s_ref[krows(kc), :]
            eq = jnp.where(s == tau, 1.0, 0.0)
            rank = seen + jnp.dot(tri, eq.astype(BF16), preferred_element_type=F32)
            s_ref[krows(kc), :] = jnp.where(eq * rank > need, NEG_INF, s)
            return rank[tk - 1:tk, :]

        lax.fori_loop(0, nk, drop_body, jnp.zeros((1, tq), F32))

    def bias_body(kc, carry):
        s_ref[krows(kc), :] = jnp.where(s_ref[krows(kc), :] >= tau, 0.0, NEG_INF)
        return carry

    lax.fori_loop(0, nk, bias_body, 0)

    m_ref[...] = jnp.full(m_ref.shape, NEG_INF, F32)
    o_ref[...] = jnp.zeros(o_ref.shape, F32)

    def logit_dots(kc, dst_ref, dmax_ref):
        bias = s_ref[krows(kc), :]
        for h in range(ATT_HEADS):
            pair = h // 2
            k = ak_ref[0, krows(kc), pair * LANES:(pair + 1) * LANES]
            lg = jnp.dot(k, qa_ref[h], preferred_element_type=F32) + bias
            dst_ref[h] = lg
            dmax_ref[h:h + 1, :] = lg.max(axis=0, keepdims=True)

    def softmax_pv(src_ref, smax_ref, kc):
        for h in range(ATT_HEADS):
            vrows = slice(h * ATT_V_ROWS, (h + 1) * ATT_V_ROWS)
            m_old = m_ref[h:h + 1, :]
            m_new = jnp.maximum(m_old, smax_ref[h:h + 1, :])
            m_safe = jnp.where(m_new == NEG_INF, 0.0, m_new)
            p = jnp.exp2(src_ref[h] - m_safe)
            alpha = jnp.exp2(m_old - m_safe)
            m_ref[h:h + 1, :] = m_new
            v_t = avt_ref[0, vrows, krows(kc)]
            o_ref[vrows, :] = alpha * o_ref[vrows, :] + jnp.dot(v_t, p.astype(BF16), preferred_element_type=F32)

    def attn_body(j, carry):
        c0 = 2 * j
        logit_dots(c0 + 1, pb_ref, mxb_ref)
        softmax_pv(pa_ref, mxa_ref, c0)
        logit_dots(c0 + 2, pa_ref, mxa_ref)
        softmax_pv(pb_ref, mxb_ref, c0 + 1)
        return carry

    logit_dots(0, pa_ref, mxa_ref)
    lax.fori_loop(0, full_trips, attn_body, 0)

    @pl.when(tail_pair)
    def _():
        logit_dots(tail0 + 1, pb_ref, mxb_ref)
        softmax_pv(pa_ref, mxa_ref, tail0)
        softmax_pv(pb_ref, mxb_ref, tail0 + 1)

    @pl.when(jnp.logical_not(tail_pair))
    def _():
        softmax_pv(pa_ref, mxa_ref, tail0)

    heads_out = []
    for h in range(ATT_HEADS):
        base = h * ATT_V_ROWS
        denom = o_ref[base + ATT_HEAD_DIM:base + ATT_HEAD_DIM + 1, :]
        heads_out.append(o_ref[base:base + ATT_HEAD_DIM, :] / denom)
    out_ref[0] = jnp.concatenate(heads_out, axis=0).T.astype(BF16)


def _sparse_attention(aq_t, ak, av_t, iq_t, ik2, iw_t):
    b, t, _ = ak.shape
    tq = ATT_TQ
    assert t % tq == 0 and ATT_TQ == ATT_TK
    topk = min(TOPK_MAX, t // 4)
    qblk_t = lambda w: pl.BlockSpec((1, w, tq), lambda bi, qi: (bi, 0, qi))
    full = lambda r, c: pl.BlockSpec((1, r, c), lambda bi, qi: (bi, 0, 0))
    return pl.pallas_call(
        functools.partial(_sparse_attn_kernel, topk=topk),
        grid=(b, t // tq),
        in_specs=[qblk_t(IDX_QW), qblk_t(LANES), qblk_t(ATT_W),
                  full(t, LANES), full(t, ATT_W), full(ATT_HEADS * ATT_V_ROWS, t)],
        out_specs=pl.BlockSpec((1, tq, ATT_W), lambda bi, qi: (bi, qi, 0)),
        out_shape=jax.ShapeDtypeStruct((b, t, ATT_W), BF16),
        scratch_shapes=[
            pltpu.VMEM((t, tq), F32),
            pltpu.VMEM((IDX_HEADS, LANES, tq), BF16),
            pltpu.VMEM((ATT_HEADS, LANES, tq), BF16),
            pltpu.VMEM((ATT_HEADS * ATT_V_ROWS, tq), F32),
            pltpu.VMEM((ATT_HEADS, tq), F32),
            pltpu.VMEM((ATT_HEADS, ATT_TK, tq), F32),
            pltpu.VMEM((ATT_HEADS, ATT_TK, tq), F32),
            pltpu.VMEM((ATT_HEADS, tq), F32),
            pltpu.VMEM((ATT_HEADS, tq), F32),
            pltpu.VMEM((2, SUBLANES, tq), F32),
        ],
        compiler_params=_params("parallel", "parallel"),
        name="sparse_attn",
    )(iq_t, iw_t, aq_t, ik2, ak, av_t)


def _retention_kernel(rq_ref, rk_ref, rv_ref, rgs_ref, gain_ref, decay_ref, xi_ref, zeta_ref, cd_ref,
                      out_ref, state_ref):
    c = RET_CHUNK

    @pl.when(pl.program_id(1) == 0)
    def _():
        state_ref[...] = jnp.zeros_like(state_ref)

    def chunk_body(ci, carry):
        rows = pl.ds(pl.multiple_of(ci * c, c), c)
        heads = range(RET_HEADS)
        qk_cols = [slice(h * RET_QK_DIM, (h + 1) * RET_QK_DIM) for h in heads]
        v_slices = [slice(h * RET_V_DIM, (h + 1) * RET_V_DIM) for h in heads]
        q = [rq_ref[0, rows, qk_cols[h]] for h in heads]
        k = [rk_ref[0, rows, qk_cols[h]] for h in heads]
        v = [rv_ref[0, rows, v_slices[h]] for h in heads]
        att = [lax.dot_general(q[h], k[h], (((1,), (1,)), ((), ())), preferred_element_type=F32)
               for h in heads]
        cross = [jnp.dot(q[h], state_ref[h].astype(BF16), preferred_element_type=F32) * xi_ref[h]
                 for h in heads]
        for h in heads:
            kz_t = (k[h].astype(F32) * zeta_ref[h]).T.astype(BF16)
            state_ref[h] = (state_ref[h] * cd_ref[h, 0:1, :]
                            + jnp.dot(kz_t, v[h], preferred_element_type=F32))
        for h in heads:
            v_cols = v_slices[h]
            inner = jnp.dot((att[h] * decay_ref[h]).astype(BF16), v[h], preferred_element_type=F32)
            o = inner + cross[h]
            mu = jnp.mean(o, axis=-1, keepdims=True)
            dev = o - mu
            var = jnp.mean(dev * dev, axis=-1, keepdims=True)
            y = dev * lax.rsqrt(var + EPS) * gain_ref[:, v_cols]
            out_ref[0, rows, v_cols] = (rgs_ref[0, rows, v_cols] * y).astype(BF16)
        return carry

    lax.fori_loop(0, rq_ref.shape[1] // c, chunk_body, 0, unroll=True)


def _retention(rq, rk, rv, rgs, gn_gain):
    b, t, _ = rq.shape
    rb = min(RET_ROWS, t)
    c = RET_CHUNK
    log_gamma = jnp.log(1.0 - 2.0 ** (-5.0 - jnp.arange(RET_HEADS, dtype=F32)))
    i = jnp.arange(c, dtype=F32)
    diff = i[:, None] - i[None, :]
    decay = jnp.where(diff[None] >= 0,
                      jnp.exp(jnp.maximum(diff, 0.0)[None] * log_gamma[:, None, None]), 0.0)
    xi = jnp.exp((i + 1.0)[None, :] * log_gamma[:, None])
    zeta = jnp.exp((c - 1.0 - i)[None, :] * log_gamma[:, None])
    xi_b = jnp.broadcast_to(xi[:, :, None], (RET_HEADS, c, RET_V_DIM))
    zeta_b = jnp.broadcast_to(zeta[:, :, None], (RET_HEADS, c, RET_QK_DIM))
    cd_b = jnp.broadcast_to(jnp.exp(c * log_gamma)[:, None, None], (RET_HEADS, SUBLANES, RET_V_DIM))
    tok = lambda w: pl.BlockSpec((1, rb, w), lambda bi, ti: (bi, ti, 0))
    return pl.pallas_call(
        _retention_kernel,
        grid=(b, t // rb),
        in_specs=[tok(RET_QKW), tok(RET_QKW), tok(RET_VW), tok(RET_VW), _resident((1, RET_VW)),
                  _resident(decay.shape), _resident(xi_b.shape), _resident(zeta_b.shape),
                  _resident(cd_b.shape)],
        out_specs=tok(RET_VW),
        out_shape=jax.ShapeDtypeStruct((b, t, RET_VW), BF16),
        scratch_shapes=[pltpu.VMEM((RET_HEADS, RET_QK_DIM, RET_V_DIM), F32)],
        compiler_params=_params("parallel", "arbitrary"),
        name="retention",
    )(rq, rk, rv, rgs, gn_gain, decay, xi_b, zeta_b, cd_b)


def _mix_out_kernel(attn_ref, gret_ref, sa_ref, sb_ref, x_ref, wa_ref, wr_ref, wo_ref,
                    g_post_ref, g_pre_ref, x1_ref, h2_ref):
    y_a = jnp.dot(attn_ref[...], wa_ref[...], preferred_element_type=F32)
    y_b = jnp.dot(gret_ref[...], wr_ref[...], preferred_element_type=F32)
    merged = sa_ref[...] * y_a + sb_ref[...] * y_b
    m = jnp.dot(merged.astype(BF16), wo_ref[...], preferred_element_type=F32)
    x1 = x_ref[...] + _rms(m, g_post_ref[...])
    x1_ref[...] = x1
    h2_ref[...] = _rms(x1, g_pre_ref[...]).astype(BF16)


def _mix_out(attn, gret, sa, sb, x, w_br_attn, w_br_ret, w_out, g_post, g_pre_ffn):
    b, t, d = x.shape
    tm = min(ROW_TILE, t)
    tok = lambda w: pl.BlockSpec((None, tm, w), lambda bi, ti: (bi, ti, 0))
    return pl.pallas_call(
        _mix_out_kernel,
        grid=(b, t // tm),
        in_specs=[tok(ATT_W), tok(RET_VW), tok(d), tok(d), tok(d),
                  _resident(w_br_attn.shape), _resident(w_br_ret.shape), _resident(w_out.shape),
                  _resident((1, d)), _resident((1, d))],
        out_specs=[tok(d), tok(d)],
        out_shape=[jax.ShapeDtypeStruct((b, t, d), F32), jax.ShapeDtypeStruct((b, t, d), BF16)],
        compiler_params=_params("parallel", "parallel"),
        name="mix_out",
    )(attn, gret, sa, sb, x, w_br_attn, w_br_ret, w_out, g_post, g_pre_ffn)


def _conv_ffn_kernel(h_ref, x1_ref, wup_ref, cw_ref, cb_ref, wdn_ref, g_ref, out_ref, carry_ref, *, d_ff):
    tm = h_ref.shape[0]

    @pl.when(pl.program_id(1) == 0)
    def _():
        carry_ref[...] = jnp.zeros_like(carry_ref)

    def up_proj(c0, width):
        return (jnp.dot(h_ref[...], wup_ref[:, c0:c0 + width], preferred_element_type=F32),
                jnp.dot(h_ref[...], wup_ref[:, d_ff + c0:d_ff + c0 + width], preferred_element_type=F32))

    def shifted(u, prev, k):
        row = lax.broadcasted_iota(jnp.int32, (SUBLANES, u.shape[1]), 0)
        r = pltpu.roll(u, k, 0)
        head = r[:SUBLANES]
        for j in range(k):
            head = jnp.where(row == j, prev[SUBLANES - k + j:SUBLANES - k + j + 1, :], head)
        return jnp.concatenate([head, r[SUBLANES:]], axis=0)

    def conv(u, c0):
        cols = slice(c0, c0 + u.shape[1])
        prev = carry_ref[:, cols]
        carry_ref[:, cols] = u[tm - SUBLANES:, :]
        return (cw_ref[0:1, cols] * shifted(u, prev, 2) + cw_ref[1:2, cols] * shifted(u, prev, 1)
                + cw_ref[2:3, cols] * u + cb_ref[:, cols])

    chunks = [(c0, min(FFN_CHUNK, d_ff - c0)) for c0 in range(0, d_ff, FFN_CHUNK)]
    acc = jnp.zeros((tm, out_ref.shape[1]), F32)
    pre = up_proj(*chunks[0])
    for i, (c0, width) in enumerate(chunks):
        nxt = up_proj(*chunks[i + 1]) if i + 1 < len(chunks) else None
        gate = conv(pre[0], c0)
        up = conv(pre[1], d_ff + c0)
        act = (gate * jax.nn.sigmoid(gate) * up).astype(BF16)
        acc = acc + jnp.dot(act, wdn_ref[c0:c0 + width, :], preferred_element_type=F32)
        pre = nxt
    out_ref[...] = x1_ref[...] + _rms(acc, g_ref[...])


def _conv_ffn(h2, x1, w_up, conv_w, conv_b, w_down, g_post):
    b, t, d = x1.shape
    d_ff = w_down.shape[0]
    assert d_ff % LANES == 0 and conv_w.shape[0] == CONV_WIDTH
    tm = min(ROW_TILE, t)
    tok = pl.BlockSpec((None, tm, d), lambda bi, ti: (bi, ti, 0))
    return pl.pallas_call(
        functools.partial(_conv_ffn_kernel, d_ff=d_ff),
        grid=(b, t // tm),
        in_specs=[tok, tok, _resident(w_up.shape), _resident(conv_w.shape), _resident((1, 2 * d_ff)),
                  _resident(w_down.shape), _resident((1, d))],
        out_specs=tok,
        out_shape=jax.ShapeDtypeStruct((b, t, d), F32),
        scratch_shapes=[pltpu.VMEM((SUBLANES, 2 * d_ff), F32)],
        compiler_params=_params("parallel", "arbitrary"),
        name="conv_ffn",
    )(h2, x1, w_up, conv_w, conv_b, w_down, g_post)


def _layer(x, positions, norm_pre_mix, w_in, w_br_attn, w_br_ret, ret_gn_gain, w_out,
           norm_post_mix, norm_pre_ffn, w_ffn_up, conv_w, conv_b, w_ffn_down, norm_post_ffn):
    d = x.shape[-1]
    row = lambda v: v.reshape(1, -1)
    tables = _rope_tables(positions)
    (aq, ak, avt, iq, ik2, iw, rq, rk, rv, rgs, sa, sb) = _in_proj(
        x, row(norm_pre_mix), _pack_w_in(w_in, d), tables)
    attn = _sparse_attention(aq, ak, avt, iq, ik2, iw)
    gret = _retention(rq, rk, rv, rgs, row(ret_gn_gain))
    x1, h2 = _mix_out(attn, gret, sa, sb, x, w_br_attn.astype(BF16), w_br_ret.astype(BF16),
                      w_out.astype(BF16), row(norm_post_mix), row(norm_pre_ffn))
    return _conv_ffn(h2, x1, w_ffn_up.astype(BF16), conv_w, row(conv_b), w_ffn_down.astype(BF16),
                     row(norm_post_ffn))


def kernel(x, positions, norm_pre_mix, w_in, w_br_attn, w_br_ret, ret_gn_gain, w_out, norm_post_mix,
           norm_pre_ffn, w_ffn_up, conv_w, conv_b, w_ffn_down, norm_post_ffn):
    for l in range(w_in.shape[0]):
        x = _layer(x, positions, norm_pre_mix[l], w_in[l], w_br_attn[l], w_br_ret[l], ret_gn_gain[l],
                   w_out[l], norm_post_mix[l], norm_pre_ffn[l], w_ffn_up[l], conv_w[l], conv_b[l],
                   w_ffn_down[l], norm_post_ffn[l])
    return x
```

```python
import functools

import jax
import jax.numpy as jnp
import numpy as np
from jax import lax
from jax.experimental import pallas as pl
from jax.experimental.pallas import tpu as pltpu

EPS = 1e-6
ROPE_THETA = 10000.0
ATT_HEADS = 8
ATT_HEAD_DIM = 64
IDX_HEADS = 8
IDX_HEAD_DIM = 64
TOPK_MAX = 256
RET_HEADS = 4
RET_QK_DIM = 128
RET_V_DIM = 256
RET_CHUNK = 128
CONV_WIDTH = 3

ATT_W = ATT_HEADS * ATT_HEAD_DIM
ATT_V_ROWS = ATT_HEAD_DIM + 16
IDX_QW = IDX_HEADS * IDX_HEAD_DIM
RET_QKW = RET_HEADS * RET_QK_DIM
RET_VW = RET_HEADS * RET_V_DIM

LANES = 128
SUBLANES = 8
VMEM_LIMIT = 56 * 1024 * 1024

ROW_TILE = 512
TABLE_TILE = 1024
ATT_TQ = 256
ATT_TK = 256
CNT_ROWS = 32
SEARCH_UNROLL = 4
SEARCH_COARSE_PASSES = 10
SEARCH_WARM_TRIPS = 2
SEARCH_MAX_PASSES = 288
RET_ROWS = 512
FFN_CHUNK = 512

F32 = jnp.float32
BF16 = jnp.bfloat16
COARSE = jnp.bfloat16
NEG_INF = float("-inf")
F32_LOWEST = float(np.finfo(np.float32).min)
F32_TINY = float(np.finfo(np.float32).tiny)
LOG2_E = float(np.log2(np.e))


def _params(*semantics):
    return pltpu.CompilerParams(dimension_semantics=semantics, vmem_limit_bytes=VMEM_LIMIT)


def _resident(shape):
    zeros = (0,) * len(shape)
    return pl.BlockSpec(shape, lambda *_: zeros, pipeline_mode=pl.Buffered(1))


def _rms(x, gain):
    return x * lax.rsqrt(jnp.mean(x * x, axis=-1, keepdims=True) + EPS) * gain


def _rope_table_kernel(pos_ref, invf_ref, sgn64_ref, sgn128_ref, c64_ref, s64_ref, c128_ref, s128_ref):
    ang = pos_ref[...].astype(F32) * invf_ref[...]
    c = jnp.cos(ang)
    s = jnp.sin(ang)
    c_lo, c_hi = c[:, :64], c[:, 64:]
    s_lo, s_hi = s[:, :64], s[:, 64:]
    c64_ref[...] = jnp.concatenate([c_lo, c_lo], axis=1)
    s64_ref[...] = jnp.concatenate([s_lo, s_lo], axis=1) * sgn64_ref[...]
    c128_ref[...] = jnp.concatenate([c_hi, c_hi], axis=1)
    s128_ref[...] = jnp.concatenate([s_hi, s_hi], axis=1) * sgn128_ref[...]


def _rope_tables(positions):
    n = positions.size
    tm = min(TABLE_TILE, n)
    half64, half128 = ATT_HEAD_DIM // 2, RET_QK_DIM // 2
    f64 = ROPE_THETA ** (-(jnp.arange(half64, dtype=F32) * 2.0 / ATT_HEAD_DIM))
    f128 = ROPE_THETA ** (-(jnp.arange(half128, dtype=F32) * 2.0 / RET_QK_DIM))
    invf = jnp.concatenate([f64, f64, f128])[None, :]
    sgn64 = jnp.tile(jnp.concatenate([-jnp.ones(half64, F32), jnp.ones(half64, F32)]), 2)[None, :]
    sgn128 = jnp.concatenate([-jnp.ones(half128, F32), jnp.ones(half128, F32)])[None, :]
    row = pl.BlockSpec((tm, LANES), lambda i: (i, 0))
    const = pl.BlockSpec((1, LANES), lambda i: (0, 0))
    table = jax.ShapeDtypeStruct((n, LANES), F32)
    return pl.pallas_call(
        _rope_table_kernel,
        grid=(n // tm,),
        in_specs=[pl.BlockSpec((tm, 1), lambda i: (i, 0)), const, const, const],
        out_specs=[row, row, row, row],
        out_shape=[table, table, table, table],
        compiler_params=_params("parallel"),
        name="rope_tables",
    )(positions.reshape(n, 1), invf, sgn64, sgn128)


_C_AQ = 0
_C_AK = _C_AQ + ATT_W
_C_AV = _C_AK + ATT_W
_C_IQ = _C_AV + ATT_W
_C_IK = _C_IQ + IDX_QW
_C_IW = _C_IK + LANES
_C_RQ = _C_IW + LANES
_C_RK = _C_RQ + RET_QKW
_C_RV = _C_RK + RET_QKW
_C_RG = _C_RV + RET_VW
_C_GA = _C_RG + RET_VW


def _pack_w_in(w_in, d_model):
    splits = (ATT_W, ATT_W, ATT_W, IDX_QW, IDX_HEAD_DIM, IDX_HEADS,
              RET_QKW, RET_QKW, RET_VW, RET_VW, d_model, d_model)
    cuts = np.cumsum(splits)[:-1].tolist()
    aq, ak, av, iq, ik, iw, rq, rk, rv, rg, ga, gb = jnp.split(w_in.astype(BF16), cuts, axis=1)
    iw_pad = jnp.pad(iw, ((0, 0), (0, LANES - IDX_HEADS)))
    return jnp.concatenate([aq, ak, av, iq, ik, ik, iw_pad, rq, rk, rv, rg, ga, gb], axis=1)


def _in_proj_kernel(x_ref, g_ref, w_ref, c64_ref, s64_ref, c128_ref, s128_ref,
                    aq_ref, ak_ref, avt_ref, iq_ref, ik_ref, iw_ref,
                    rq_ref, rk_ref, rv_ref, rgs_ref, sa_ref, sb_ref, *, d_model):
    tm = x_ref.shape[0]
    h = _rms(x_ref[...], g_ref[...]).astype(BF16)

    def proj(c0, width):
        return jnp.dot(h, w_ref[:, c0:c0 + width], preferred_element_type=F32)

    lane = lax.broadcasted_iota(jnp.int32, (tm, LANES), 1)
    first_half = (lane % ATT_HEAD_DIM) < (ATT_HEAD_DIM // 2)
    c64, s64 = c64_ref[...], s64_ref[...]
    c128, s128 = c128_ref[...], s128_ref[...]

    def rot64(y):
        out = []
        for j in range(y.shape[1] // LANES):
            yj = y[:, j * LANES:(j + 1) * LANES]
            partner = jnp.where(first_half, pltpu.roll(yj, LANES - 32, 1), pltpu.roll(yj, 32, 1))
            out.append(yj * c64 + partner * s64)
        return out[0] if len(out) == 1 else jnp.concatenate(out, axis=1)

    def rot128(y):
        out = []
        for j in range(y.shape[1] // LANES):
            yj = y[:, j * LANES:(j + 1) * LANES]
            out.append(yj * c128 + pltpu.roll(yj, 64, 1) * s128)
        return jnp.concatenate(out, axis=1)

    aq_ref[0] = (rot64(proj(_C_AQ, ATT_W)) * (ATT_HEAD_DIM ** -0.5 * LOG2_E)).T.astype(BF16)
    ak_ref[0] = rot64(proj(_C_AK, ATT_W)).astype(BF16)
    v_t = proj(_C_AV, ATT_W).T
    ones = jnp.ones((ATT_V_ROWS - ATT_HEAD_DIM, tm), F32)
    avt_ref[0] = jnp.concatenate(
        [part for h in range(ATT_HEADS) for part in (v_t[h * ATT_HEAD_DIM:(h + 1) * ATT_HEAD_DIM], ones)],
        axis=0).astype(BF16)
    iq_ref[0] = rot64(proj(_C_IQ, IDX_QW)).T.astype(BF16)
    ik_ref[0] = rot64(proj(_C_IK, LANES)).astype(BF16)
    iw_ref[0] = (proj(_C_IW, LANES) * (IDX_HEADS ** -0.5 * IDX_HEAD_DIM ** -0.5)).T
    rq_ref[0] = rot128(proj(_C_RQ, RET_QKW)).astype(BF16)
    rk_ref[0] = (rot128(proj(_C_RK, RET_QKW)) * (RET_QK_DIM ** -0.5)).astype(BF16)
    rv_ref[0] = proj(_C_RV, RET_VW).astype(BF16)
    rg = proj(_C_RG, RET_VW)
    rgs_ref[0] = rg * jax.nn.sigmoid(rg)
    sa_ref[0] = jax.nn.sigmoid(proj(_C_GA, d_model))
    sb_ref[0] = jax.nn.sigmoid(proj(_C_GA + d_model, d_model))


def _in_proj(x, gain, w_packed, tables):
    b, t, d = x.shape
    tm = min(ROW_TILE, t)
    nt = t // tm
    width = w_packed.shape[1]
    tok = lambda w: pl.BlockSpec((1, tm, w), lambda bi, ti: (bi, ti, 0))
    tab = pl.BlockSpec((tm, LANES), lambda bi, ti: (bi * nt + ti, 0))
    sds = lambda w, dt: jax.ShapeDtypeStruct((b, t, w), dt)
    tok_t = lambda w: pl.BlockSpec((1, w, tm), lambda bi, ti: (bi, 0, ti))
    sds_t = lambda w, dt: jax.ShapeDtypeStruct((b, w, t), dt)
    outs = [
        (tok_t(ATT_W), sds_t(ATT_W, BF16)),
        (tok(ATT_W), sds(ATT_W, BF16)),
        (tok_t(ATT_HEADS * ATT_V_ROWS), sds_t(ATT_HEADS * ATT_V_ROWS, BF16)),
        (tok_t(IDX_QW), sds_t(IDX_QW, BF16)),
        (tok(LANES), sds(LANES, BF16)),
        (tok_t(LANES), sds_t(LANES, F32)),
        (tok(RET_QKW), sds(RET_QKW, BF16)),
        (tok(RET_QKW), sds(RET_QKW, BF16)),
        (tok(RET_VW), sds(RET_VW, BF16)),
        (tok(RET_VW), sds(RET_VW, F32)),
        (tok(d), sds(d, F32)),
        (tok(d), sds(d, F32)),
    ]
    return pl.pallas_call(
        functools.partial(_in_proj_kernel, d_model=d),
        grid=(b, nt),
        in_specs=[pl.BlockSpec((None, tm, d), lambda bi, ti: (bi, ti, 0)),
                  _resident((1, d)), _resident((d, width)), tab, tab, tab, tab],
        out_specs=[o[0] for o in outs],
        out_shape=[o[1] for o in outs],
        compiler_params=_params("parallel", "parallel"),
        name="in_proj",
    )(x, gain, w_packed, *tables)


def _sparse_attn_kernel(iq_ref, iw_ref, aq_ref, ik_ref, ak_ref, avt_ref, out_ref,
                        s_ref, sb_ref, qi_ref, qa_ref, o_ref, m_ref, pa_ref, pb_ref, mxa_ref, mxb_ref, bnd_ref,
                        *, topk):
    tq, tk = ATT_TQ, ATT_TK
    qi = pl.program_id(1)
    nk = qi + 1

    def krows(kc):
        return pl.ds(pl.multiple_of(kc * tk, tk), tk)

    zeros = jnp.zeros((ATT_HEAD_DIM, tq), BF16)
    for h in range(ATT_HEADS):
        hrows = slice(h * ATT_HEAD_DIM, (h + 1) * ATT_HEAD_DIM)
        halves = [zeros, aq_ref[0, hrows, :]] if h % 2 else [aq_ref[0, hrows, :], zeros]
        qa_ref[h] = jnp.concatenate(halves, axis=0)
        qi_ref[h] = jnp.concatenate([iq_ref[0, hrows, :], zeros], axis=0)
    w_t = iw_ref[0]

    full_trips = (nk - 1) // 2
    tail0 = 2 * full_trips
    tail_pair = (nk - tail0) == 2

    def score_dots(kc, dst_ref):
        k = ik_ref[0, krows(kc), :]
        for h in range(IDX_HEADS):
            dst_ref[h] = jnp.dot(k, qi_ref[h], preferred_element_type=F32)

    def score_reduce(src_ref, kc, diagonal=False):
        acc = None
        for h in range(IDX_HEADS):
            term = w_t[h:h + 1, :] * jnp.maximum(src_ref[h], 0.0)
            acc = term if acc is None else acc + term
        groups = acc.reshape(tk // SUBLANES, SUBLANES, tq)
        bnd_ref[0] = jnp.maximum(bnd_ref[0], groups.max(axis=0))
        bnd_ref[1] = jnp.minimum(bnd_ref[1], groups.min(axis=0))
        if diagonal:
            acc = jnp.where(lax.broadcasted_iota(jnp.int32, (tk, tq), 0)
                            <= lax.broadcasted_iota(jnp.int32, (tk, tq), 1), acc, NEG_INF)
        s_ref[krows(kc), :] = acc
        sb_ref[krows(kc), :] = acc.astype(COARSE)
        for slot, bound in ((2, 0.0), (3, F32_TINY)):
            hits = jnp.where(acc >= bound, 1.0, 0.0).reshape(tk // SUBLANES, SUBLANES, tq)
            bnd_ref[slot] = bnd_ref[slot] + hits.sum(axis=0)

    def score_body(j, carry):
        c0 = 2 * j
        score_dots(c0 + 1, pb_ref)
        score_reduce(pa_ref, c0)
        score_dots(c0 + 2, pa_ref)
        score_reduce(pb_ref, c0 + 1)
        return carry

    bnd_ref[0] = jnp.full((SUBLANES, tq), NEG_INF, F32)
    bnd_ref[1] = jnp.full((SUBLANES, tq), -NEG_INF, F32)
    bnd_ref[2] = jnp.zeros((SUBLANES, tq), F32)
    bnd_ref[3] = jnp.zeros((SUBLANES, tq), F32)
    score_dots(0, pa_ref)
    lax.fori_loop(0, full_trips, score_body, 0)

    @pl.when(tail_pair)
    def _():
        score_dots(tail0 + 1, pb_ref)
        score_reduce(pa_ref, tail0)
        score_reduce(pb_ref, tail0 + 1, diagonal=True)

    @pl.when(jnp.logical_not(tail_pair))
    def _():
        score_reduce(pa_ref, tail0, diagonal=True)

    smax = bnd_ref[0].max(axis=0, keepdims=True)
    smin = bnd_ref[1].min(axis=0, keepdims=True)

    def count(hits):
        def body(kc, acc):
            hit = hits(s_ref[krows(kc), :], kc)
            return acc + hit.reshape(tk // CNT_ROWS, CNT_ROWS, tq).sum(axis=0)
        acc = lax.fori_loop(0, nk, body, jnp.zeros((CNT_ROWS, tq), jnp.int32))
        return acc.sum(axis=0, keepdims=True)

    def count_ge(cand_f):
        def body(kc, acc):
            s = s_ref[krows(kc), :]
            for g in range(tk // CNT_ROWS):
                acc = jnp.where(s[g * CNT_ROWS:(g + 1) * CNT_ROWS] >= cand_f, acc + 1, acc)
            return acc
        acc = lax.fori_loop(0, nk, body, jnp.zeros((CNT_ROWS, tq), jnp.int32))
        return acc.sum(axis=0, keepdims=True)

    def flag(cond):
        return jnp.where(cond, 1.0, 0.0)

    n_keys = qi * tq + lax.broadcasted_iota(jnp.int32, (1, tq), 1) + 1
    at_zero = bnd_ref[2].sum(axis=0, keepdims=True).astype(jnp.int32)
    above_zero = bnd_ref[3].sum(axis=0, keepdims=True).astype(jnp.int32)
    positive = above_zero >= topk
    negative = at_zero < topk
    lo = jnp.where(positive, F32_TINY, jnp.where(negative, smin, 0.0))
    cnt_lo = jnp.where(positive, above_zero, jnp.where(negative, n_keys, at_zero))
    hi = jnp.where(positive, 2.0 * smax, jnp.where(negative, -0.0, 0.0))
    few = n_keys < topk
    settled = jnp.maximum(flag(few), (1.0 - flag(positive)) * (1.0 - flag(negative)))

    def cell_floor(c):
        bits = lax.bitcast_convert_type(c, jnp.int32)
        odd = jnp.where((bits & 0x10000) == 0, 0, 1)
        edge = jnp.where(bits > 0, bits - 0x8000 + odd, bits + 0x8000 - odd)
        return lax.bitcast_convert_type(edge, F32)

    def count_ge_bf16(cand):
        def body(kc, acc):
            s = sb_ref[krows(kc), :]
            for g in range(tk // CNT_ROWS):
                acc = jnp.where(s[g * CNT_ROWS:(g + 1) * CNT_ROWS] >= cand, acc + 1, acc)
            return acc
        acc = lax.fori_loop(0, nk, body, jnp.zeros((CNT_ROWS, tq), COARSE))
        return acc.astype(F32).sum(axis=0, keepdims=True).astype(jnp.int32)

    def coarse_body(_, carry):
        lo, hi, cnt_lo = carry
        cand = (0.5 * lo + 0.5 * hi).astype(COARSE)
        cand_f = cand.astype(F32)
        edge = cell_floor(cand_f)
        tot = count_ge_bf16(cand)
        inside = flag(edge > lo) * flag(edge < hi) * flag(cand_f != 0.0)
        up = inside * flag(tot >= topk) > 0.5
        down = inside * flag(tot < topk) > 0.5
        return jnp.where(up, edge, lo), jnp.where(down, edge, hi), jnp.where(up, tot, cnt_lo)

    lo, hi, cnt_lo = lax.fori_loop(0, SEARCH_COARSE_PASSES, coarse_body, (lo, hi, cnt_lo))

    def search_cond(carry):
        p, _, _, _, done = carry
        return (p < SEARCH_MAX_PASSES) & (done < 0.5)

    def bisect(lo, hi, cnt_lo):
        closed = jnp.zeros((1, tq), F32)
        for _ in range(SEARCH_UNROLL):
            mid = 0.5 * lo + 0.5 * hi
            tot = count_ge(mid)
            ok = tot >= topk
            closed = jnp.maximum(closed, jnp.maximum(flag(mid <= lo), flag(mid >= hi)))
            lo = jnp.where(ok, mid, lo)
            cnt_lo = jnp.where(ok, tot, cnt_lo)
            hi = jnp.where(ok, hi, mid)
        return lo, hi, cnt_lo, closed

    def warm_body(_, carry):
        return bisect(*carry)[:3]

    def search_body(carry):
        p, lo, hi, cnt_lo, _ = carry
        lo, hi, cnt_lo, closed = bisect(lo, hi, cnt_lo)
        done = jnp.min(jnp.maximum(jnp.maximum(settled, closed), flag(cnt_lo == topk)))
        return p + SEARCH_UNROLL, lo, hi, cnt_lo, done

    lo, hi, cnt_lo = lax.fori_loop(0, SEARCH_WARM_TRIPS, warm_body, (lo, hi, cnt_lo))
    done0 = jnp.min(jnp.maximum(settled, flag(cnt_lo == topk)))
    _, lo, _, cnt_lo, _ = lax.while_loop(
        search_cond, search_body, (jnp.int32(SEARCH_WARM_TRIPS * SEARCH_UNROLL), lo, hi, cnt_lo, done0))
    tau = jnp.where(few, F32_LOWEST, lo)

    tied = (1.0 - flag(few)) * flag(cnt_lo > topk)

    @pl.when(jnp.max(tied) > 0.5)
    def _():
        need = (topk - count(lambda s, kc: jnp.where(s > tau, 1, 0))).astype(F32)
        tri = jnp.where(lax.broadcasted_iota(jnp.int32, (tk, tk), 1)
                        <= lax.broadcasted_iota(jnp.int32, (tk, tk), 0), 1.0, 0.0).astype(BF16)

        def drop_body(kc, seen):
            s = s_ref[krows(kc), :]
            eq = jnp.where(s == tau, 1.0, 0.0)
            rank = seen + jnp.dot(tri, eq.astype(BF16), preferred_element_type=F32)
            s_ref[krows(kc), :] = jnp.where(eq * rank > need, NEG_INF, s)
            return rank[tk - 1:tk, :]

        lax.fori_loop(0, nk, drop_body, jnp.zeros((1, tq), F32))

    def bias_body(kc, carry):
        s_ref[krows(kc), :] = jnp.where(s_ref[krows(kc), :] >= tau, 0.0, NEG_INF)
        return carry

    lax.fori_loop(0, nk, bias_body, 0)

    m_ref[...] = jnp.full(m_ref.shape, NEG_INF, F32)
    o_ref[...] = jnp.zeros(o_ref.shape, F32)

    def logit_dots(kc, dst_ref, dmax_ref):
        bias = s_ref[krows(kc), :]
        for h in range(ATT_HEADS):
            pair = h // 2
            k = ak_ref[0, krows(kc), pair * LANES:(pair + 1) * LANES]
            lg = jnp.dot(k, qa_ref[h], preferred_element_type=F32) + bias
            dst_ref[h] = lg
            dmax_ref[h:h + 1, :] = lg.max(axis=0, keepdims=True)

    def softmax_pv(src_ref, smax_ref, kc):
        for h in range(ATT_HEADS):
            vrows = slice(h * ATT_V_ROWS, (h + 1) * ATT_V_ROWS)
            m_old = m_ref[h:h + 1, :]
            m_new = jnp.maximum(m_old, smax_ref[h:h + 1, :])
            m_safe = jnp.where(m_new == NEG_INF, 0.0, m_new)
            p = jnp.exp2(src_ref[h] - m_safe)
            alpha = jnp.exp2(m_old - m_safe)
            m_ref[h:h + 1, :] = m_new
            v_t = avt_ref[0, vrows, krows(kc)]
            o_ref[vrows, :] = alpha * o_ref[vrows, :] + jnp.dot(v_t, p.astype(BF16), preferred_element_type=F32)

    def attn_body(j, carry):
        c0 = 2 * j
        logit_dots(c0 + 1, pb_ref, mxb_ref)
        softmax_pv(pa_ref, mxa_ref, c0)
        logit_dots(c0 + 2, pa_ref, mxa_ref)
        softmax_pv(pb_ref, mxb_ref, c0 + 1)
        return carry

    logit_dots(0, pa_ref, mxa_ref)
    lax.fori_loop(0, full_trips, attn_body, 0)

    @pl.when(tail_pair)
    def _():
        logit_dots(tail0 + 1, pb_ref, mxb_ref)
        softmax_pv(pa_ref, mxa_ref, tail0)
        softmax_pv(pb_ref, mxb_ref, tail0 + 1)

    @pl.when(jnp.logical_not(tail_pair))
    def _():
        softmax_pv(pa_ref, mxa_ref, tail0)

    heads_out = []
    for h in range(ATT_HEADS):
        base = h * ATT_V_ROWS
        denom = o_ref[base + ATT_HEAD_DIM:base + ATT_HEAD_DIM + 1, :]
        heads_out.append(o_ref[base:base + ATT_HEAD_DIM, :] / denom)
    out_ref[0] = jnp.concatenate(heads_out, axis=0).T.astype(BF16)


def _sparse_attention(aq_t, ak, av_t, iq_t, ik2, iw_t):
    b, t, _ = ak.shape
    tq = ATT_TQ
    assert t % tq == 0 and ATT_TQ == ATT_TK
    assert t // CNT_ROWS <= 256
    topk = min(TOPK_MAX, t // 4)
    qblk_t = lambda w: pl.BlockSpec((1, w, tq), lambda bi, qi: (bi, 0, qi))
    full = lambda r, c: pl.BlockSpec((1, r, c), lambda bi, qi: (bi, 0, 0))
    return pl.pallas_call(
        functools.partial(_sparse_attn_kernel, topk=topk),
        grid=(b, t // tq),
        in_specs=[qblk_t(IDX_QW), qblk_t(LANES), qblk_t(ATT_W),
                  full(t, LANES), full(t, ATT_W), full(ATT_HEADS * ATT_V_ROWS, t)],
        out_specs=pl.BlockSpec((1, tq, ATT_W), lambda bi, qi: (bi, qi, 0)),
        out_shape=jax.ShapeDtypeStruct((b, t, ATT_W), BF16),
        scratch_shapes=[
            pltpu.VMEM((t, tq), F32),
            pltpu.VMEM((t, tq), COARSE),
            pltpu.VMEM((IDX_HEADS, LANES, tq), BF16),
            pltpu.VMEM((ATT_HEADS, LANES, tq), BF16),
            pltpu.VMEM((ATT_HEADS * ATT_V_ROWS, tq), F32),
            pltpu.VMEM((ATT_HEADS, tq), F32),
            pltpu.VMEM((ATT_HEADS, ATT_TK, tq), F32),
            pltpu.VMEM((ATT_HEADS, ATT_TK, tq), F32),
            pltpu.VMEM((ATT_HEADS, tq), F32),
            pltpu.VMEM((ATT_HEADS, tq), F32),
            pltpu.VMEM((4, SUBLANES, tq), F32),
        ],
        compiler_params=_params("parallel", "parallel"),
        name="sparse_attn",
    )(iq_t, iw_t, aq_t, ik2, ak, av_t)


def _retention_kernel(rq_ref, rk_ref, rv_ref, rgs_ref, gain_ref, decay_ref, xi_ref, zeta_ref, cd_ref,
                      out_ref, state_ref):
    c = RET_CHUNK

    @pl.when(pl.program_id(1) == 0)
    def _():
        state_ref[...] = jnp.zeros_like(state_ref)

    def chunk_body(ci, carry):
        rows = pl.ds(pl.multiple_of(ci * c, c), c)
        heads = range(RET_HEADS)
        qk_cols = [slice(h * RET_QK_DIM, (h + 1) * RET_QK_DIM) for h in heads]
        v_slices = [slice(h * RET_V_DIM, (h + 1) * RET_V_DIM) for h in heads]
        q = [rq_ref[0, rows, qk_cols[h]] for h in heads]
        k = [rk_ref[0, rows, qk_cols[h]] for h in heads]
        v = [rv_ref[0, rows, v_slices[h]] for h in heads]
        att = [lax.dot_general(q[h], k[h], (((1,), (1,)), ((), ())), preferred_element_type=F32)
               for h in heads]
        cross = [jnp.dot(q[h], state_ref[h].astype(BF16), preferred_element_type=F32) * xi_ref[h]
                 for h in heads]
        for h in heads:
            kz_t = (k[h].astype(F32) * zeta_ref[h]).T.astype(BF16)
            state_ref[h] = (state_ref[h] * cd_ref[h, 0:1, :]
                            + jnp.dot(kz_t, v[h], preferred_element_type=F32))
        for h in heads:
            v_cols = v_slices[h]
            inner = jnp.dot((att[h] * decay_ref[h]).astype(BF16), v[h], preferred_element_type=F32)
            o = inner + cross[h]
            mu = jnp.mean(o, axis=-1, keepdims=True)
            dev = o - mu
            var = jnp.mean(dev * dev, axis=-1, keepdims=True)
            y = dev * lax.rsqrt(var + EPS) * gain_ref[:, v_cols]
            out_ref[0, rows, v_cols] = (rgs_ref[0, rows, v_cols] * y).astype(BF16)
        return carry

    lax.fori_loop(0, rq_ref.shape[1] // c, chunk_body, 0, unroll=True)


def _retention(rq, rk, rv, rgs, gn_gain):
    b, t, _ = rq.shape
    rb = min(RET_ROWS, t)
    c = RET_CHUNK
    log_gamma = jnp.log(1.0 - 2.0 ** (-5.0 - jnp.arange(RET_HEADS, dtype=F32)))
    i = jnp.arange(c, dtype=F32)
    diff = i[:, None] - i[None, :]
    decay = jnp.where(diff[None] >= 0,
                      jnp.exp(jnp.maximum(diff, 0.0)[None] * log_gamma[:, None, None]), 0.0)
    xi = jnp.exp((i + 1.0)[None, :] * log_gamma[:, None])
    zeta = jnp.exp((c - 1.0 - i)[None, :] * log_gamma[:, None])
    xi_b = jnp.broadcast_to(xi[:, :, None], (RET_HEADS, c, RET_V_DIM))
    zeta_b = jnp.broadcast_to(zeta[:, :, None], (RET_HEADS, c, RET_QK_DIM))
    cd_b = jnp.broadcast_to(jnp.exp(c * log_gamma)[:, None, None], (RET_HEADS, SUBLANES, RET_V_DIM))
    tok = lambda w: pl.BlockSpec((1, rb, w), lambda bi, ti: (bi, ti, 0))
    return pl.pallas_call(
        _retention_kernel,
        grid=(b, t // rb),
        in_specs=[tok(RET_QKW), tok(RET_QKW), tok(RET_VW), tok(RET_VW), _resident((1, RET_VW)),
                  _resident(decay.shape), _resident(xi_b.shape), _resident(zeta_b.shape),
                  _resident(cd_b.shape)],
        out_specs=tok(RET_VW),
        out_shape=jax.ShapeDtypeStruct((b, t, RET_VW), BF16),
        scratch_shapes=[pltpu.VMEM((RET_HEADS, RET_QK_DIM, RET_V_DIM), F32)],
        compiler_params=_params("parallel", "arbitrary"),
        name="retention",
    )(rq, rk, rv, rgs, gn_gain, decay, xi_b, zeta_b, cd_b)


def _mix_out_kernel(attn_ref, gret_ref, sa_ref, sb_ref, x_ref, wa_ref, wr_ref, wo_ref,
                    g_post_ref, g_pre_ref, x1_ref, h2_ref):
    y_a = jnp.dot(attn_ref[...], wa_ref[...], preferred_element_type=F32)
    y_b = jnp.dot(gret_ref[...], wr_ref[...], preferred_element_type=F32)
    merged = sa_ref[...] * y_a + sb_ref[...] * y_b
    m = jnp.dot(merged.astype(BF16), wo_ref[...], preferred_element_type=F32)
    x1 = x_ref[...] + _rms(m, g_post_ref[...])
    x1_ref[...] = x1
    h2_ref[...] = _rms(x1, g_pre_ref[...]).astype(BF16)


def _mix_out(attn, gret, sa, sb, x, w_br_attn, w_br_ret, w_out, g_post, g_pre_ffn):
    b, t, d = x.shape
    tm = min(ROW_TILE, t)
    tok = lambda w: pl.BlockSpec((None, tm, w), lambda bi, ti: (bi, ti, 0))
    return pl.pallas_call(
        _mix_out_kernel,
        grid=(b, t // tm),
        in_specs=[tok(ATT_W), tok(RET_VW), tok(d), tok(d), tok(d),
                  _resident(w_br_attn.shape), _resident(w_br_ret.shape), _resident(w_out.shape),
                  _resident((1, d)), _resident((1, d))],
        out_specs=[tok(d), tok(d)],
        out_shape=[jax.ShapeDtypeStruct((b, t, d), F32), jax.ShapeDtypeStruct((b, t, d), BF16)],
        compiler_params=_params("parallel", "parallel"),
        name="mix_out",
    )(attn, gret, sa, sb, x, w_br_attn, w_br_ret, w_out, g_post, g_pre_ffn)


def _conv_ffn_kernel(h_ref, x1_ref, wup_ref, cw_ref, cb_ref, wdn_ref, g_ref, out_ref, carry_ref, *, d_ff):
    tm = h_ref.shape[0]

    @pl.when(pl.program_id(1) == 0)
    def _():
        carry_ref[...] = jnp.zeros_like(carry_ref)

    def up_proj(c0, width):
        return (jnp.dot(h_ref[...], wup_ref[:, c0:c0 + width], preferred_element_type=F32),
                jnp.dot(h_ref[...], wup_ref[:, d_ff + c0:d_ff + c0 + width], preferred_element_type=F32))

    def shifted(u, prev, k):
        row = lax.broadcasted_iota(jnp.int32, (SUBLANES, u.shape[1]), 0)
        r = pltpu.roll(u, k, 0)
        head = r[:SUBLANES]
        for j in range(k):
            head = jnp.where(row == j, prev[SUBLANES - k + j:SUBLANES - k + j + 1, :], head)
        return jnp.concatenate([head, r[SUBLANES:]], axis=0)

    def conv(u, c0):
        cols = slice(c0, c0 + u.shape[1])
        prev = carry_ref[:, cols]
        carry_ref[:, cols] = u[tm - SUBLANES:, :]
        return (cw_ref[0:1, cols] * shifted(u, prev, 2) + cw_ref[1:2, cols] * shifted(u, prev, 1)
                + cw_ref[2:3, cols] * u + cb_ref[:, cols])

    chunks = [(c0, min(FFN_CHUNK, d_ff - c0)) for c0 in range(0, d_ff, FFN_CHUNK)]
    acc = jnp.zeros((tm, out_ref.shape[1]), F32)
    pre = up_proj(*chunks[0])
    for i, (c0, width) in enumerate(chunks):
        nxt = up_proj(*chunks[i + 1]) if i + 1 < len(chunks) else None
        gate = conv(pre[0], c0)
        up = conv(pre[1], d_ff + c0)
        act = (gate * jax.nn.sigmoid(gate) * up).astype(BF16)
        acc = acc + jnp.dot(act, wdn_ref[c0:c0 + width, :], preferred_element_type=F32)
        pre = nxt
    out_ref[...] = x1_ref[...] + _rms(acc, g_ref[...])


def _conv_ffn(h2, x1, w_up, conv_w, conv_b, w_down, g_post):
    b, t, d = x1.shape
    d_ff = w_down.shape[0]
    assert d_ff % LANES == 0 and conv_w.shape[0] == CONV_WIDTH
    tm = min(ROW_TILE, t)
    tok = pl.BlockSpec((None, tm, d), lambda bi, ti: (bi, ti, 0))
    return pl.pallas_call(
        functools.partial(_conv_ffn_kernel, d_ff=d_ff),
        grid=(b, t // tm),
        in_specs=[tok, tok, _resident(w_up.shape), _resident(conv_w.shape), _resident((1, 2 * d_ff)),
                  _resident(w_down.shape), _resident((1, d))],
        out_specs=tok,
        out_shape=jax.ShapeDtypeStruct((b, t, d), F32),
        scratch_shapes=[pltpu.VMEM((SUBLANES, 2 * d_ff), F32)],
        compiler_params=_params("parallel", "arbitrary"),
        name="conv_ffn",
    )(h2, x1, w_up, conv_w, conv_b, w_down, g_post)


def _layer(x, positions, norm_pre_mix, w_in, w_br_attn, w_br_ret, ret_gn_gain, w_out,
           norm_post_mix, norm_pre_ffn, w_ffn_up, conv_w, conv_b, w_ffn_down, norm_post_ffn):
    d = x.shape[-1]
    row = lambda v: v.reshape(1, -1)
    tables = _rope_tables(positions)
    (aq, ak, avt, iq, ik2, iw, rq, rk, rv, rgs, sa, sb) = _in_proj(
        x, row(norm_pre_mix), _pack_w_in(w_in, d), tables)
    attn = _sparse_attention(aq, ak, avt, iq, ik2, iw)
    gret = _retention(rq, rk, rv, rgs, row(ret_gn_gain))
    x1, h2 = _mix_out(attn, gret, sa, sb, x, w_br_attn.astype(BF16), w_br_ret.astype(BF16),
                      w_out.astype(BF16), row(norm_post_mix), row(norm_pre_ffn))
    return _conv_ffn(h2, x1, w_ffn_up.astype(BF16), conv_w, row(conv_b), w_ffn_down.astype(BF16),
                     row(norm_post_ffn))


def kernel(x, positions, norm_pre_mix, w_in, w_br_attn, w_br_ret, ret_gn_gain, w_out, norm_post_mix,
           norm_pre_ffn, w_ffn_up, conv_w, conv_b, w_ffn_down, norm_post_ffn):
    for l in range(w_in.shape[0]):
        x = _layer(x, positions, norm_pre_mix[l], w_in[l], w_br_attn[l], w_br_ret[l], ret_gn_gain[l],
                   w_out[l], norm_post_mix[l], norm_pre_ffn[l], w_ffn_up[l], conv_w[l], conv_b[l],
                   w_ffn_down[l], norm_post_ffn[l])
    return x
```

```python
import functools

import jax
import jax.numpy as jnp
import numpy as np
from jax import lax
from jax.experimental import pallas as pl
from jax.experimental.pallas import tpu as pltpu

EPS = 1e-6
ROPE_THETA = 10000.0
ATT_HEADS = 8
ATT_HEAD_DIM = 64
IDX_HEADS = 8
IDX_HEAD_DIM = 64
TOPK_MAX = 256
RET_HEADS = 4
RET_QK_DIM = 128
RET_V_DIM = 256
RET_CHUNK = 128
CONV_WIDTH = 3

ATT_W = ATT_HEADS * ATT_HEAD_DIM
ATT_V_ROWS = ATT_HEAD_DIM + 16
IDX_QW = IDX_HEADS * IDX_HEAD_DIM
RET_QKW = RET_HEADS * RET_QK_DIM
RET_VW = RET_HEADS * RET_V_DIM

LANES = 128
SUBLANES = 8
VMEM_LIMIT = 56 * 1024 * 1024

ROW_TILE = 512
TABLE_TILE = 1024
ATT_TQ = 256
ATT_TK = 256
CNT_ROWS = 32
SEARCH_UNROLL = 2
SEARCH_COARSE_PASSES = 10
SEARCH_WARM_TRIPS = 4
SEARCH_MAX_PASSES = 288
RET_ROWS = 512
FFN_CHUNK = 1408

F32 = jnp.float32
BF16 = jnp.bfloat16
COARSE = jnp.bfloat16
NEG_INF = float("-inf")
F32_LOWEST = float(np.finfo(np.float32).min)
F32_TINY = float(np.finfo(np.float32).tiny)
LOG2_E = float(np.log2(np.e))


def _params(*semantics):
    return pltpu.CompilerParams(dimension_semantics=semantics, vmem_limit_bytes=VMEM_LIMIT)


def _resident(shape):
    zeros = (0,) * len(shape)
    return pl.BlockSpec(shape, lambda *_: zeros, pipeline_mode=pl.Buffered(1))


def _rms(x, gain):
    return x * lax.rsqrt(jnp.mean(x * x, axis=-1, keepdims=True) + EPS) * gain


def _rope_table_kernel(pos_ref, invf_ref, sgn64_ref, sgn128_ref, c64_ref, s64_ref, c128_ref, s128_ref):
    ang = pos_ref[...].astype(F32) * invf_ref[...]
    c = jnp.cos(ang)
    s = jnp.sin(ang)
    c_lo, c_hi = c[:, :64], c[:, 64:]
    s_lo, s_hi = s[:, :64], s[:, 64:]
    c64_ref[...] = jnp.concatenate([c_lo, c_lo], axis=1)
    s64_ref[...] = jnp.concatenate([s_lo, s_lo], axis=1) * sgn64_ref[...]
    c128_ref[...] = jnp.concatenate([c_hi, c_hi], axis=1)
    s128_ref[...] = jnp.concatenate([s_hi, s_hi], axis=1) * sgn128_ref[...]


def _rope_tables(positions):
    n = positions.size
    tm = min(TABLE_TILE, n)
    half64, half128 = ATT_HEAD_DIM // 2, RET_QK_DIM // 2
    f64 = ROPE_THETA ** (-(jnp.arange(half64, dtype=F32) * 2.0 / ATT_HEAD_DIM))
    f128 = ROPE_THETA ** (-(jnp.arange(half128, dtype=F32) * 2.0 / RET_QK_DIM))
    invf = jnp.concatenate([f64, f64, f128])[None, :]
    sgn64 = jnp.tile(jnp.concatenate([-jnp.ones(half64, F32), jnp.ones(half64, F32)]), 2)[None, :]
    sgn128 = jnp.concatenate([-jnp.ones(half128, F32), jnp.ones(half128, F32)])[None, :]
    row = pl.BlockSpec((tm, LANES), lambda i: (i, 0))
    const = pl.BlockSpec((1, LANES), lambda i: (0, 0))
    table = jax.ShapeDtypeStruct((n, LANES), F32)
    return pl.pallas_call(
        _rope_table_kernel,
        grid=(n // tm,),
        in_specs=[pl.BlockSpec((tm, 1), lambda i: (i, 0)), const, const, const],
        out_specs=[row, row, row, row],
        out_shape=[table, table, table, table],
        compiler_params=_params("parallel"),
        name="rope_tables",
    )(positions.reshape(n, 1), invf, sgn64, sgn128)


_C_AQ = 0
_C_AK = _C_AQ + ATT_W
_C_AV = _C_AK + ATT_W
_C_IQ = _C_AV + ATT_W
_C_IK = _C_IQ + IDX_QW
_C_IW = _C_IK + LANES
_C_RQ = _C_IW + LANES
_C_RK = _C_RQ + RET_QKW
_C_RV = _C_RK + RET_QKW
_C_RG = _C_RV + RET_VW
_C_GA = _C_RG + RET_VW


def _pack_w_in(w_in, d_model):
    splits = (ATT_W, ATT_W, ATT_W, IDX_QW, IDX_HEAD_DIM, IDX_HEADS,
              RET_QKW, RET_QKW, RET_VW, RET_VW, d_model, d_model)
    cuts = np.cumsum(splits)[:-1].tolist()
    aq, ak, av, iq, ik, iw, rq, rk, rv, rg, ga, gb = jnp.split(w_in.astype(BF16), cuts, axis=1)
    iw_pad = jnp.pad(iw, ((0, 0), (0, LANES - IDX_HEADS)))
    return jnp.concatenate([aq, ak, av, iq, ik, ik, iw_pad, rq, rk, rv, rg, ga, gb], axis=1)


def _in_proj_kernel(x_ref, g_ref, w_ref, c64_ref, s64_ref, c128_ref, s128_ref,
                    aq_ref, ak_ref, avt_ref, iq_ref, ik_ref, iw_ref,
                    rq_ref, rk_ref, rv_ref, rgs_ref, sa_ref, sb_ref, *, d_model):
    tm = x_ref.shape[0]
    h = _rms(x_ref[...], g_ref[...]).astype(BF16)

    def proj(c0, width):
        return jnp.dot(h, w_ref[:, c0:c0 + width], preferred_element_type=F32)

    lane = lax.broadcasted_iota(jnp.int32, (tm, LANES), 1)
    first_half = (lane % ATT_HEAD_DIM) < (ATT_HEAD_DIM // 2)
    c64, s64 = c64_ref[...], s64_ref[...]
    c128, s128 = c128_ref[...], s128_ref[...]

    def rot64(y):
        out = []
        for j in range(y.shape[1] // LANES):
            yj = y[:, j * LANES:(j + 1) * LANES]
            partner = jnp.where(first_half, pltpu.roll(yj, LANES - 32, 1), pltpu.roll(yj, 32, 1))
            out.append(yj * c64 + partner * s64)
        return out[0] if len(out) == 1 else jnp.concatenate(out, axis=1)

    def rot128(y):
        out = []
        for j in range(y.shape[1] // LANES):
            yj = y[:, j * LANES:(j + 1) * LANES]
            out.append(yj * c128 + pltpu.roll(yj, 64, 1) * s128)
        return jnp.concatenate(out, axis=1)

    aq_ref[0] = (rot64(proj(_C_AQ, ATT_W)) * (ATT_HEAD_DIM ** -0.5 * LOG2_E)).T.astype(BF16)
    ak_ref[0] = rot64(proj(_C_AK, ATT_W)).astype(BF16)
    v_t = proj(_C_AV, ATT_W).T
    ones = jnp.ones((ATT_V_ROWS - ATT_HEAD_DIM, tm), F32)
    avt_ref[0] = jnp.concatenate(
        [part for h in range(ATT_HEADS) for part in (v_t[h * ATT_HEAD_DIM:(h + 1) * ATT_HEAD_DIM], ones)],
        axis=0).astype(BF16)
    iq_ref[0] = rot64(proj(_C_IQ, IDX_QW)).T.astype(BF16)
    ik_ref[0] = rot64(proj(_C_IK, LANES)).astype(BF16)
    iw_ref[0] = (proj(_C_IW, LANES) * (IDX_HEADS ** -0.5 * IDX_HEAD_DIM ** -0.5)).T
    rq_ref[0] = rot128(proj(_C_RQ, RET_QKW)).astype(BF16)
    rk_ref[0] = (rot128(proj(_C_RK, RET_QKW)) * (RET_QK_DIM ** -0.5)).astype(BF16)
    rv_ref[0] = proj(_C_RV, RET_VW).astype(BF16)
    rg = proj(_C_RG, RET_VW)
    rgs_ref[0] = rg * jax.nn.sigmoid(rg)
    sa_ref[0] = jax.nn.sigmoid(proj(_C_GA, d_model))
    sb_ref[0] = jax.nn.sigmoid(proj(_C_GA + d_model, d_model))


def _in_proj(x, gain, w_packed, tables):
    b, t, d = x.shape
    tm = min(ROW_TILE, t)
    nt = t // tm
    width = w_packed.shape[1]
    tok = lambda w: pl.BlockSpec((1, tm, w), lambda bi, ti: (bi, ti, 0))
    tab = pl.BlockSpec((tm, LANES), lambda bi, ti: (bi * nt + ti, 0))
    sds = lambda w, dt: jax.ShapeDtypeStruct((b, t, w), dt)
    tok_t = lambda w: pl.BlockSpec((1, w, tm), lambda bi, ti: (bi, 0, ti))
    sds_t = lambda w, dt: jax.ShapeDtypeStruct((b, w, t), dt)
    outs = [
        (tok_t(ATT_W), sds_t(ATT_W, BF16)),
        (tok(ATT_W), sds(ATT_W, BF16)),
        (tok_t(ATT_HEADS * ATT_V_ROWS), sds_t(ATT_HEADS * ATT_V_ROWS, BF16)),
        (tok_t(IDX_QW), sds_t(IDX_QW, BF16)),
        (tok(LANES), sds(LANES, BF16)),
        (tok_t(LANES), sds_t(LANES, F32)),
        (tok(RET_QKW), sds(RET_QKW, BF16)),
        (tok(RET_QKW), sds(RET_QKW, BF16)),
        (tok(RET_VW), sds(RET_VW, BF16)),
        (tok(RET_VW), sds(RET_VW, F32)),
        (tok(d), sds(d, F32)),
        (tok(d), sds(d, F32)),
    ]
    return pl.pallas_call(
        functools.partial(_in_proj_kernel, d_model=d),
        grid=(b, nt),
        in_specs=[pl.BlockSpec((None, tm, d), lambda bi, ti: (bi, ti, 0)),
                  _resident((1, d)), _resident((d, width)), tab, tab, tab, tab],
        out_specs=[o[0] for o in outs],
        out_shape=[o[1] for o in outs],
        compiler_params=_params("parallel", "parallel"),
        name="in_proj",
    )(x, gain, w_packed, *tables)


def _sparse_attn_kernel(iq_ref, iw_ref, aq_ref, ik_ref, ak_ref, avt_ref, out_ref,
                        s_ref, sb_ref, qi_ref, qa_ref, o_ref, m_ref, pa_ref, pb_ref, mxa_ref, mxb_ref, bnd_ref,
                        *, topk):
    tq, tk = ATT_TQ, ATT_TK
    qi = pl.program_id(1)
    nk = qi + 1

    def krows(kc):
        return pl.ds(pl.multiple_of(kc * tk, tk), tk)

    zeros = jnp.zeros((ATT_HEAD_DIM, tq), BF16)
    for h in range(ATT_HEADS):
        hrows = slice(h * ATT_HEAD_DIM, (h + 1) * ATT_HEAD_DIM)
        halves = [zeros, aq_ref[0, hrows, :]] if h % 2 else [aq_ref[0, hrows, :], zeros]
        qa_ref[h] = jnp.concatenate(halves, axis=0)
        qi_ref[h] = jnp.concatenate([iq_ref[0, hrows, :], zeros], axis=0)
    w_t = iw_ref[0]

    full_trips = (nk - 1) // 2
    tail0 = 2 * full_trips
    tail_pair = (nk - tail0) == 2

    def score_dots(kc, dst_ref):
        k = ik_ref[0, krows(kc), :]
        for h in range(IDX_HEADS):
            dst_ref[h] = jnp.dot(k, qi_ref[h], preferred_element_type=F32)

    def score_reduce(src_ref, kc, diagonal=False):
        acc = None
        for h in range(IDX_HEADS):
            term = w_t[h:h + 1, :] * jnp.maximum(src_ref[h], 0.0)
            acc = term if acc is None else acc + term
        groups = acc.reshape(tk // SUBLANES, SUBLANES, tq)
        bnd_ref[0] = jnp.maximum(bnd_ref[0], groups.max(axis=0))
        bnd_ref[1] = jnp.minimum(bnd_ref[1], groups.min(axis=0))
        if diagonal:
            acc = jnp.where(lax.broadcasted_iota(jnp.int32, (tk, tq), 0)
                            <= lax.broadcasted_iota(jnp.int32, (tk, tq), 1), acc, NEG_INF)
        s_ref[krows(kc), :] = acc
        sb_ref[krows(kc), :] = acc.astype(COARSE)
        for slot, bound in ((2, 0.0), (3, F32_TINY)):
            hits = jnp.where(acc >= bound, 1.0, 0.0).reshape(tk // SUBLANES, SUBLANES, tq)
            bnd_ref[slot] = bnd_ref[slot] + hits.sum(axis=0)

    def score_body(j, carry):
        c0 = 2 * j
        score_dots(c0 + 1, pb_ref)
        score_reduce(pa_ref, c0)
        score_dots(c0 + 2, pa_ref)
        score_reduce(pb_ref, c0 + 1)
        return carry

    bnd_ref[0] = jnp.full((SUBLANES, tq), NEG_INF, F32)
    bnd_ref[1] = jnp.full((SUBLANES, tq), -NEG_INF, F32)
    bnd_ref[2] = jnp.zeros((SUBLANES, tq), F32)
    bnd_ref[3] = jnp.zeros((SUBLANES, tq), F32)
    score_dots(0, pa_ref)
    lax.fori_loop(0, full_trips, score_body, 0)

    @pl.when(tail_pair)
    def _():
        score_dots(tail0 + 1, pb_ref)
        score_reduce(pa_ref, tail0)
        score_reduce(pb_ref, tail0 + 1, diagonal=True)

    @pl.when(jnp.logical_not(tail_pair))
    def _():
        score_reduce(pa_ref, tail0, diagonal=True)

    smax = bnd_ref[0].max(axis=0, keepdims=True)
    smin = bnd_ref[1].min(axis=0, keepdims=True)

    def count(hits):
        def body(kc, acc):
            hit = hits(s_ref[krows(kc), :], kc)
            return acc + hit.reshape(tk // CNT_ROWS, CNT_ROWS, tq).sum(axis=0)
        acc = lax.fori_loop(0, nk, body, jnp.zeros((CNT_ROWS, tq), jnp.int32))
        return acc.sum(axis=0, keepdims=True)

    def count_ge(cand_f):
        def body(kc, acc):
            s = s_ref[krows(kc), :]
            for g in range(tk // CNT_ROWS):
                acc = jnp.where(s[g * CNT_ROWS:(g + 1) * CNT_ROWS] >= cand_f, acc + 1, acc)
            return acc
        acc = lax.fori_loop(0, nk, body, jnp.zeros((CNT_ROWS, tq), jnp.int32))
        return acc.sum(axis=0, keepdims=True)

    def flag(cond):
        return jnp.where(cond, 1.0, 0.0)

    n_keys = qi * tq + lax.broadcasted_iota(jnp.int32, (1, tq), 1) + 1
    at_zero = bnd_ref[2].sum(axis=0, keepdims=True).astype(jnp.int32)
    above_zero = bnd_ref[3].sum(axis=0, keepdims=True).astype(jnp.int32)
    positive = above_zero >= topk
    negative = at_zero < topk
    lo = jnp.where(positive, F32_TINY, jnp.where(negative, smin, 0.0))
    cnt_lo = jnp.where(positive, above_zero, jnp.where(negative, n_keys, at_zero))
    hi = jnp.where(positive, 2.0 * smax, jnp.where(negative, -0.0, 0.0))
    few = n_keys < topk
    settled = jnp.maximum(flag(few), (1.0 - flag(positive)) * (1.0 - flag(negative)))

    def cell_floor(c):
        bits = lax.bitcast_convert_type(c, jnp.int32)
        odd = jnp.where((bits & 0x10000) == 0, 0, 1)
        edge = jnp.where(bits > 0, bits - 0x8000 + odd, bits + 0x8000 - odd)
        return lax.bitcast_convert_type(edge, F32)

    def count_ge_bf16(cand):
        def body(kc, acc):
            s = sb_ref[krows(kc), :]
            for g in range(tk // CNT_ROWS):
                acc = jnp.where(s[g * CNT_ROWS:(g + 1) * CNT_ROWS] >= cand, acc + 1, acc)
            return acc
        acc = lax.fori_loop(0, nk, body, jnp.zeros((CNT_ROWS, tq), COARSE))
        return acc.astype(F32).sum(axis=0, keepdims=True).astype(jnp.int32)

    def coarse_body(_, carry):
        lo, hi, cnt_lo = carry
        cand = (0.5 * lo + 0.5 * hi).astype(COARSE)
        cand_f = cand.astype(F32)
        edge = cell_floor(cand_f)
        tot = count_ge_bf16(cand)
        inside = flag(edge > lo) * flag(edge < hi) * flag(cand_f != 0.0)
        up = inside * flag(tot >= topk) > 0.5
        down = inside * flag(tot < topk) > 0.5
        return jnp.where(up, edge, lo), jnp.where(down, edge, hi), jnp.where(up, tot, cnt_lo)

    lo, hi, cnt_lo = lax.fori_loop(0, SEARCH_COARSE_PASSES, coarse_body, (lo, hi, cnt_lo))

    def search_cond(carry):
        p, _, _, _, done = carry
        return (p < SEARCH_MAX_PASSES) & (done < 0.5)

    def bisect(lo, hi, cnt_lo):
        closed = jnp.zeros((1, tq), F32)
        for _ in range(SEARCH_UNROLL):
            mid = 0.5 * lo + 0.5 * hi
            tot = count_ge(mid)
            ok = tot >= topk
            closed = jnp.maximum(closed, jnp.maximum(flag(mid <= lo), flag(mid >= hi)))
            lo = jnp.where(ok, mid, lo)
            cnt_lo = jnp.where(ok, tot, cnt_lo)
            hi = jnp.where(ok, hi, mid)
        return lo, hi, cnt_lo, closed

    def warm_body(_, carry):
        return bisect(*carry)[:3]

    def search_body(carry):
        p, lo, hi, cnt_lo, _ = carry
        lo, hi, cnt_lo, closed = bisect(lo, hi, cnt_lo)
        done = jnp.min(jnp.maximum(jnp.maximum(settled, closed), flag(cnt_lo == topk)))
        return p + SEARCH_UNROLL, lo, hi, cnt_lo, done

    lo, hi, cnt_lo = lax.fori_loop(0, SEARCH_WARM_TRIPS, warm_body, (lo, hi, cnt_lo))
    done0 = jnp.min(jnp.maximum(settled, flag(cnt_lo == topk)))
    _, lo, _, cnt_lo, _ = lax.while_loop(
        search_cond, search_body, (jnp.int32(SEARCH_WARM_TRIPS * SEARCH_UNROLL), lo, hi, cnt_lo, done0))
    tau = jnp.where(few, F32_LOWEST, lo)

    tied = (1.0 - flag(few)) * flag(cnt_lo > topk)

    @pl.when(jnp.max(tied) > 0.5)
    def _():
        need = (topk - count(lambda s, kc: jnp.where(s > tau, 1, 0))).astype(F32)
        tri = jnp.where(lax.broadcasted_iota(jnp.int32, (tk, tk), 1)
                        <= lax.broadcasted_iota(jnp.int32, (tk, tk), 0), 1.0, 0.0).astype(BF16)

        def drop_body(kc, seen):
            s = s_ref[krows(kc), :]
            eq = jnp.where(s == tau, 1.0, 0.0)
            rank = seen + jnp.dot(tri, eq.astype(BF16), preferred_element_type=F32)
            s_ref[krows(kc), :] = jnp.where(eq * rank > need, NEG_INF, s)
            return rank[tk - 1:tk, :]

        lax.fori_loop(0, nk, drop_body, jnp.zeros((1, tq), F32))

    def bias_body(kc, carry):
        s_ref[krows(kc), :] = jnp.where(s_ref[krows(kc), :] >= tau, 0.0, NEG_INF)
        return carry

    lax.fori_loop(0, nk, bias_body, 0)

    m_ref[...] = jnp.full(m_ref.shape, NEG_INF, F32)
    o_ref[...] = jnp.zeros(o_ref.shape, F32)

    def logit_dots(kc, dst_ref, dmax_ref):
        bias = s_ref[krows(kc), :]
        for h in range(ATT_HEADS):
            pair = h // 2
            k = ak_ref[0, krows(kc), pair * LANES:(pair + 1) * LANES]
            lg = jnp.dot(k, qa_ref[h], preferred_element_type=F32) + bias
            dst_ref[h] = lg
            dmax_ref[h:h + 1, :] = lg.max(axis=0, keepdims=True)

    def softmax_pv(src_ref, smax_ref, kc):
        for h in range(ATT_HEADS):
            vrows = slice(h * ATT_V_ROWS, (h + 1) * ATT_V_ROWS)
            m_old = m_ref[h:h + 1, :]
            m_new = jnp.maximum(m_old, smax_ref[h:h + 1, :])
            m_safe = jnp.where(m_new == NEG_INF, 0.0, m_new)
            p = jnp.exp2(src_ref[h] - m_safe)
            alpha = jnp.exp2(m_old - m_safe)
            m_ref[h:h + 1, :] = m_new
            v_t = avt_ref[0, vrows, krows(kc)]
            o_ref[vrows, :] = alpha * o_ref[vrows, :] + jnp.dot(v_t, p.astype(BF16), preferred_element_type=F32)

    def attn_body(j, carry):
        c0 = 2 * j
        logit_dots(c0 + 1, pb_ref, mxb_ref)
        softmax_pv(pa_ref, mxa_ref, c0)
        logit_dots(c0 + 2, pa_ref, mxa_ref)
        softmax_pv(pb_ref, mxb_ref, c0 + 1)
        return carry

    logit_dots(0, pa_ref, mxa_ref)
    lax.fori_loop(0, full_trips, attn_body, 0)

    @pl.when(tail_pair)
    def _():
        logit_dots(tail0 + 1, pb_ref, mxb_ref)
        softmax_pv(pa_ref, mxa_ref, tail0)
        softmax_pv(pb_ref, mxb_ref, tail0 + 1)

    @pl.when(jnp.logical_not(tail_pair))
    def _():
        softmax_pv(pa_ref, mxa_ref, tail0)

    heads_out = []
    for h in range(ATT_HEADS):
        base = h * ATT_V_ROWS
        denom = o_ref[base + ATT_HEAD_DIM:base + ATT_HEAD_DIM + 1, :]
        heads_out.append(o_ref[base:base + ATT_HEAD_DIM, :] / denom)
    out_ref[0] = jnp.concatenate(heads_out, axis=0).T.astype(BF16)


def _sparse_attention(aq_t, ak, av_t, iq_t, ik2, iw_t):
    b, t, _ = ak.shape
    tq = ATT_TQ
    assert t % tq == 0 and ATT_TQ == ATT_TK
    assert t // CNT_ROWS <= 256
    topk = min(TOPK_MAX, t // 4)
    qblk_t = lambda w: pl.BlockSpec((1, w, tq), lambda bi, qi: (bi, 0, qi))
    full = lambda r, c: pl.BlockSpec((1, r, c), lambda bi, qi: (bi, 0, 0))
    return pl.pallas_call(
        functools.partial(_sparse_attn_kernel, topk=topk),
        grid=(b, t // tq),
        in_specs=[qblk_t(IDX_QW), qblk_t(LANES), qblk_t(ATT_W),
                  full(t, LANES), full(t, ATT_W), full(ATT_HEADS * ATT_V_ROWS, t)],
        out_specs=pl.BlockSpec((1, tq, ATT_W), lambda bi, qi: (bi, qi, 0)),
        out_shape=jax.ShapeDtypeStruct((b, t, ATT_W), BF16),
        scratch_shapes=[
            pltpu.VMEM((t, tq), F32),
            pltpu.VMEM((t, tq), COARSE),
            pltpu.VMEM((IDX_HEADS, LANES, tq), BF16),
            pltpu.VMEM((ATT_HEADS, LANES, tq), BF16),
            pltpu.VMEM((ATT_HEADS * ATT_V_ROWS, tq), F32),
            pltpu.VMEM((ATT_HEADS, tq), F32),
            pltpu.VMEM((ATT_HEADS, ATT_TK, tq), F32),
            pltpu.VMEM((ATT_HEADS, ATT_TK, tq), F32),
            pltpu.VMEM((ATT_HEADS, tq), F32),
            pltpu.VMEM((ATT_HEADS, tq), F32),
            pltpu.VMEM((4, SUBLANES, tq), F32),
        ],
        compiler_params=_params("parallel", "parallel"),
        name="sparse_attn",
    )(iq_t, iw_t, aq_t, ik2, ak, av_t)


def _retention_kernel(rq_ref, rk_ref, rv_ref, rgs_ref, gain_ref, decay_ref, xi_ref, zeta_ref, cd_ref,
                      out_ref, state_ref):
    c = RET_CHUNK

    @pl.when(pl.program_id(1) == 0)
    def _():
        state_ref[...] = jnp.zeros_like(state_ref)

    def chunk_body(ci, carry):
        rows = pl.ds(pl.multiple_of(ci * c, c), c)
        heads = range(RET_HEADS)
        qk_cols = [slice(h * RET_QK_DIM, (h + 1) * RET_QK_DIM) for h in heads]
        v_slices = [slice(h * RET_V_DIM, (h + 1) * RET_V_DIM) for h in heads]
        q = [rq_ref[0, rows, qk_cols[h]] for h in heads]
        k = [rk_ref[0, rows, qk_cols[h]] for h in heads]
        v = [rv_ref[0, rows, v_slices[h]] for h in heads]
        att = [lax.dot_general(q[h], k[h], (((1,), (1,)), ((), ())), preferred_element_type=F32)
               for h in heads]
        cross = [jnp.dot(q[h], state_ref[h].astype(BF16), preferred_element_type=F32) * xi_ref[h]
                 for h in heads]
        for h in heads:
            kz_t = (k[h].astype(F32) * zeta_ref[h]).T.astype(BF16)
            state_ref[h] = (state_ref[h] * cd_ref[h, 0:1, :]
                            + jnp.dot(kz_t, v[h], preferred_element_type=F32))
        for h in heads:
            v_cols = v_slices[h]
            inner = jnp.dot((att[h] * decay_ref[h]).astype(BF16), v[h], preferred_element_type=F32)
            o = inner + cross[h]
            mu = jnp.mean(o, axis=-1, keepdims=True)
            dev = o - mu
            var = jnp.mean(dev * dev, axis=-1, keepdims=True)
            y = dev * lax.rsqrt(var + EPS) * gain_ref[:, v_cols]
            out_ref[0, rows, v_cols] = (rgs_ref[0, rows, v_cols] * y).astype(BF16)
        return carry

    lax.fori_loop(0, rq_ref.shape[1] // c, chunk_body, 0, unroll=True)


def _retention(rq, rk, rv, rgs, gn_gain):
    b, t, _ = rq.shape
    rb = min(RET_ROWS, t)
    c = RET_CHUNK
    log_gamma = jnp.log(1.0 - 2.0 ** (-5.0 - jnp.arange(RET_HEADS, dtype=F32)))
    i = jnp.arange(c, dtype=F32)
    diff = i[:, None] - i[None, :]
    decay = jnp.where(diff[None] >= 0,
                      jnp.exp(jnp.maximum(diff, 0.0)[None] * log_gamma[:, None, None]), 0.0)
    xi = jnp.exp((i + 1.0)[None, :] * log_gamma[:, None])
    zeta = jnp.exp((c - 1.0 - i)[None, :] * log_gamma[:, None])
    xi_b = jnp.broadcast_to(xi[:, :, None], (RET_HEADS, c, RET_V_DIM))
    zeta_b = jnp.broadcast_to(zeta[:, :, None], (RET_HEADS, c, RET_QK_DIM))
    cd_b = jnp.broadcast_to(jnp.exp(c * log_gamma)[:, None, None], (RET_HEADS, SUBLANES, RET_V_DIM))
    tok = lambda w: pl.BlockSpec((1, rb, w), lambda bi, ti: (bi, ti, 0))
    return pl.pallas_call(
        _retention_kernel,
        grid=(b, t // rb),
        in_specs=[tok(RET_QKW), tok(RET_QKW), tok(RET_VW), tok(RET_VW), _resident((1, RET_VW)),
                  _resident(decay.shape), _resident(xi_b.shape), _resident(zeta_b.shape),
                  _resident(cd_b.shape)],
        out_specs=tok(RET_VW),
        out_shape=jax.ShapeDtypeStruct((b, t, RET_VW), BF16),
        scratch_shapes=[pltpu.VMEM((RET_HEADS, RET_QK_DIM, RET_V_DIM), F32)],
        compiler_params=_params("parallel", "arbitrary"),
        name="retention",
    )(rq, rk, rv, rgs, gn_gain, decay, xi_b, zeta_b, cd_b)


def _mix_out_kernel(attn_ref, gret_ref, sa_ref, sb_ref, x_ref, wa_ref, wr_ref, wo_ref,
                    g_post_ref, g_pre_ref, x1_ref, h2_ref):
    y_a = jnp.dot(attn_ref[...], wa_ref[...], preferred_element_type=F32)
    y_b = jnp.dot(gret_ref[...], wr_ref[...], preferred_element_type=F32)
    merged = sa_ref[...] * y_a + sb_ref[...] * y_b
    m = jnp.dot(merged.astype(BF16), wo_ref[...], preferred_element_type=F32)
    x1 = x_ref[...] + _rms(m, g_post_ref[...])
    x1_ref[...] = x1
    h2_ref[...] = _rms(x1, g_pre_ref[...]).astype(BF16)


def _mix_out(attn, gret, sa, sb, x, w_br_attn, w_br_ret, w_out, g_post, g_pre_ffn):
    b, t, d = x.shape
    tm = min(ROW_TILE, t)
    tok = lambda w: pl.BlockSpec((None, tm, w), lambda bi, ti: (bi, ti, 0))
    return pl.pallas_call(
        _mix_out_kernel,
        grid=(b, t // tm),
        in_specs=[tok(ATT_W), tok(RET_VW), tok(d), tok(d), tok(d),
                  _resident(w_br_attn.shape), _resident(w_br_ret.shape), _resident(w_out.shape),
                  _resident((1, d)), _resident((1, d))],
        out_specs=[tok(d), tok(d)],
        out_shape=[jax.ShapeDtypeStruct((b, t, d), F32), jax.ShapeDtypeStruct((b, t, d), BF16)],
        compiler_params=_params("parallel", "parallel"),
        name="mix_out",
    )(attn, gret, sa, sb, x, w_br_attn, w_br_ret, w_out, g_post, g_pre_ffn)


def _conv_ffn_kernel(h_ref, x1_ref, wup_ref, cw_ref, cb_ref, wdn_ref, g_ref, out_ref, carry_ref, *, d_ff):
    tm = h_ref.shape[0]

    @pl.when(pl.program_id(1) == 0)
    def _():
        carry_ref[...] = jnp.zeros_like(carry_ref)

    def up_proj(c0, width):
        return (jnp.dot(h_ref[...], wup_ref[:, c0:c0 + width], preferred_element_type=F32),
                jnp.dot(h_ref[...], wup_ref[:, d_ff + c0:d_ff + c0 + width], preferred_element_type=F32))

    def shifted(u, prev, k):
        row = lax.broadcasted_iota(jnp.int32, (SUBLANES, u.shape[1]), 0)
        r = pltpu.roll(u, k, 0)
        head = r[:SUBLANES]
        for j in range(k):
            head = jnp.where(row == j, prev[SUBLANES - k + j:SUBLANES - k + j + 1, :], head)
        return jnp.concatenate([head, r[SUBLANES:]], axis=0)

    def conv(u, c0):
        cols = slice(c0, c0 + u.shape[1])
        prev = carry_ref[:, cols]
        carry_ref[:, cols] = u[tm - SUBLANES:, :]
        return (cw_ref[0:1, cols] * shifted(u, prev, 2) + cw_ref[1:2, cols] * shifted(u, prev, 1)
                + cw_ref[2:3, cols] * u + cb_ref[:, cols])

    chunks = [(c0, min(FFN_CHUNK, d_ff - c0)) for c0 in range(0, d_ff, FFN_CHUNK)]
    acc = jnp.zeros((tm, out_ref.shape[1]), F32)
    pre = up_proj(*chunks[0])
    for i, (c0, width) in enumerate(chunks):
        nxt = up_proj(*chunks[i + 1]) if i + 1 < len(chunks) else None
        gate = conv(pre[0], c0)
        up = conv(pre[1], d_ff + c0)
        act = (gate * jax.nn.sigmoid(gate) * up).astype(BF16)
        acc = acc + jnp.dot(act, wdn_ref[c0:c0 + width, :], preferred_element_type=F32)
        pre = nxt
    out_ref[...] = x1_ref[...] + _rms(acc, g_ref[...])


def _conv_ffn(h2, x1, w_up, conv_w, conv_b, w_down, g_post):
    b, t, d = x1.shape
    d_ff = w_down.shape[0]
    assert d_ff % LANES == 0 and conv_w.shape[0] == CONV_WIDTH
    tm = min(ROW_TILE, t)
    tok = pl.BlockSpec((None, tm, d), lambda bi, ti: (bi, ti, 0))
    return pl.pallas_call(
        functools.partial(_conv_ffn_kernel, d_ff=d_ff),
        grid=(b, t // tm),
        in_specs=[tok, tok, _resident(w_up.shape), _resident(conv_w.shape), _resident((1, 2 * d_ff)),
                  _resident(w_down.shape), _resident((1, d))],
        out_specs=tok,
        out_shape=jax.ShapeDtypeStruct((b, t, d), F32),
        scratch_shapes=[pltpu.VMEM((SUBLANES, 2 * d_ff), F32)],
        compiler_params=_params("parallel", "arbitrary"),
        name="conv_ffn",
    )(h2, x1, w_up, conv_w, conv_b, w_down, g_post)


def _layer(x, positions, norm_pre_mix, w_in, w_br_attn, w_br_ret, ret_gn_gain, w_out,
           norm_post_mix, norm_pre_ffn, w_ffn_up, conv_w, conv_b, w_ffn_down, norm_post_ffn):
    d = x.shape[-1]
    row = lambda v: v.reshape(1, -1)
    tables = _rope_tables(positions)
    (aq, ak, avt, iq, ik2, iw, rq, rk, rv, rgs, sa, sb) = _in_proj(
        x, row(norm_pre_mix), _pack_w_in(w_in, d), tables)
    attn = _sparse_attention(aq, ak, avt, iq, ik2, iw)
    gret = _retention(rq, rk, rv, rgs, row(ret_gn_gain))
    x1, h2 = _mix_out(attn, gret, sa, sb, x, w_br_attn.astype(BF16), w_br_ret.astype(BF16),
                      w_out.astype(BF16), row(norm_post_mix), row(norm_pre_ffn))
    return _conv_ffn(h2, x1, w_ffn_up.astype(BF16), conv_w, row(conv_b), w_ffn_down.astype(BF16),
                     row(norm_post_ffn))


def kernel(x, positions, norm_pre_mix, w_in, w_br_attn, w_br_ret, ret_gn_gain, w_out, norm_post_mix,
           norm_pre_ffn, w_ffn_up, conv_w, conv_b, w_ffn_down, norm_post_ffn):
    for l in range(w_in.shape[0]):
        x = _layer(x, positions, norm_pre_mix[l], w_in[l], w_br_attn[l], w_br_ret[l], ret_gn_gain[l],
                   w_out[l], norm_post_mix[l], norm_pre_ffn[l], w_ffn_up[l], conv_w[l], conv_b[l],
                   w_ffn_down[l], norm_post_ffn[l])
    return x
```
